```python
import jax, jax.numpy as jnp
from jax import lax
import numpy as np

D_MODEL = 1024
BATCH = 2
SEQ = 8192
DEPTH = 2

GRID_W = 64
CTX_LEN = 256
N_MIXERS = 2
N_MOD = 6
LN_EPS = 1e-5
ALPHA = (2 * DEPTH) ** 0.25
BETA = (8 * DEPTH) ** -0.25
RW_HEAD = 64
RW_HEADS = D_MODEL // RW_HEAD
RW_DECAY_LORA = 64
RW_A_LORA = 64
RW_GATE_LORA = 128
RW_LNX_EPS = 64e-5
RW_DECAY_SCALE = 0.606531
ATT_HEAD = 64
ATT_Q_HEADS = D_MODEL // ATT_HEAD
ATT_KV_HEADS = 4
ATT_GROUP = ATT_Q_HEADS // ATT_KV_HEADS
WINDOW = 128
BLOCK = 128
ROPE_BASE = 10000.0
NEG = -1e30
N_EXPERTS = 16
EC_CAPACITY = 2
D_EXPERT = 2 * D_MODEL
N_RWKV = (DEPTH + 1) // 2
N_ATTN = DEPTH // 2

kernel_name = "hybrid_rwkv7_swa_sink_ecmoe_deepnorm_dit"


def layer_norm(x, g, b):
    xf = x.astype(jnp.float32)
    mu = xf.mean(-1, keepdims=True)
    var = jnp.square(xf - mu).mean(-1, keepdims=True)
    return ((xf - mu) * lax.rsqrt(var + LN_EPS)).astype(x.dtype) * g + b


def modulate(x, shift, scale):
    return x * (1 + scale) + shift


def centred_shift(x):
    zero = jnp.zeros_like(x[:, :1])
    prev = jnp.concatenate([zero, x[:, :-1]], axis=1)
    nxt = jnp.concatenate([x[:, 1:], zero], axis=1)
    return 0.5 * (prev + nxt) - x


def rwkv_inputs(h, mu, wr, wk, wv, w0, w1, w2, a0, a1, a2, g1, g2, k_k, k_a):
    B, T, D = h.shape
    heads = lambda t: t.reshape(B, T, RW_HEADS, RW_HEAD).astype(jnp.float32)
    xx = centred_shift(h)
    xr, xw, xk, xv, xa, xg = [h + xx * mu[m] for m in range(6)]
    r = xr @ wr
    k = xk @ wk
    v = xv @ wv
    g = jax.nn.sigmoid(xg @ g1) @ g2
    kk = heads(k * k_k)
    kk = kk / jnp.maximum(jnp.linalg.norm(kk, axis=-1, keepdims=True), 1e-12)
    dirs = []
    for d in range(2):
        w_d = jnp.exp(-RW_DECAY_SCALE * jax.nn.sigmoid(w0[d] + jnp.tanh(xw @ w1[d]) @ w2[d]))
        a_d = jax.nn.sigmoid(a0[d] + (xa @ a1[d]) @ a2[d])
        k_d = k * (1 + (a_d - 1) * k_a)
        a_h = heads(a_d)
        dirs.append((heads(w_d), heads(k_d), -kk, kk * a_h))
    return heads(r), heads(k), heads(v), g, dirs


def rwkv_scan(state, r, w, k, v, a, b, reverse):
    xs = tuple(jnp.swapaxes(t, 0, 1) for t in (r, w, k, v, a, b))

    def step(S, inp):
        r_t, w_t, k_t, v_t, a_t, b_t = inp
        sa = jnp.einsum('bhvk,bhk->bhv', S, a_t)
        S = (S * w_t[:, :, None, :] + sa[..., :, None] * b_t[..., None, :]
             + v_t[..., :, None] * k_t[..., None, :])
        return S, jnp.einsum('bhvk,bhk->bhv', S, r_t)

    S, ys = lax.scan(step, state, xs, reverse=reverse)
    return S, jnp.swapaxes(ys, 0, 1)


def rwkv_output(y, r, k, v, g, r_k, lnx_g, lnx_b, wo):
    B, T = y.shape[:2]
    mu = y.mean(-1, keepdims=True)
    var = jnp.square(y - mu).mean(-1, keepdims=True)
    yn = ((y - mu) * lax.rsqrt(var + RW_LNX_EPS)).reshape(B, T, D_MODEL) * lnx_g + lnx_b
    bonus = (jnp.sum(r * k * r_k, axis=-1, keepdims=True) * v).reshape(B, T, D_MODEL)
    return ((yn + bonus).astype(g.dtype) * g) @ wo


def rwkv_mixer(h, hc, mu, wr, wk, wv, w0, w1, w2, a0, a1, a2, g1, g2, k_k, k_a,
               r_k, lnx_g, lnx_b, wo, need_ctx_out):
    prm = (mu, wr, wk, wv, w0, w1, w2, a0, a1, a2, g1, g2, k_k, k_a)
    r, k, v, g, dirs = rwkv_inputs(h, *prm)
    rc, kc, vc, gc, dirs_c = rwkv_inputs(hc, *prm)
    S0 = jnp.zeros((h.shape[0], RW_HEADS, RW_HEAD, RW_HEAD), jnp.float32)
    y_lat, y_ctx = 0.0, 0.0
    for d, rev in enumerate((False, True)):
        wc_d, kc_d, ac_d, bc_d = dirs_c[d]
        Sc, yc = rwkv_scan(S0, rc, wc_d, kc_d, vc, ac_d, bc_d, rev)
        w_d, k_d, a_d, b_d = dirs[d]
        _, yl = rwkv_scan(Sc, r, w_d, k_d, v, a_d, b_d, rev)
        y_lat = y_lat + yl
        y_ctx = y_ctx + yc
    out = rwkv_output(y_lat, r, k, v, g, r_k, lnx_g, lnx_b, wo)
    out_c = rwkv_output(y_ctx, rc, kc, vc, gc, r_k, lnx_g, lnx_b, wo) if need_ctx_out else None
    return out, out_c


def rotate(xs, ang):
    x1, x2 = jnp.split(xs, 2, axis=-1)
    c, s = jnp.cos(ang).astype(xs.dtype), jnp.sin(ang).astype(xs.dtype)
    return jnp.concatenate([x1 * c - x2 * s, x2 * c + x1 * s], axis=-1)


def axial_rope(x, rows, cols):
    half = x.shape[-1] // 2
    inv = 1.0 / (ROPE_BASE ** (jnp.arange(0, half, 2, dtype=jnp.float32) / half))
    shp = (1, x.shape[1]) + (1,) * (x.ndim - 3) + (half // 2,)
    ang_r = (rows.astype(jnp.float32)[:, None] * inv).reshape(shp)
    ang_c = (cols.astype(jnp.float32)[:, None] * inv).reshape(shp)
    return jnp.concatenate([rotate(x[..., :half], ang_r), rotate(x[..., half:], ang_c)], axis=-1)


def qkv_split(h, wqkv):
    B, T, _ = h.shape
    qkv = h @ wqkv
    nq, nk = ATT_Q_HEADS * ATT_HEAD, ATT_KV_HEADS * ATT_HEAD
    q = qkv[..., :nq].reshape(B, T, ATT_KV_HEADS, ATT_GROUP, ATT_HEAD)
    k = qkv[..., nq:nq + nk].reshape(B, T, ATT_KV_HEADS, ATT_HEAD)
    v = qkv[..., nq + nk:].reshape(B, T, ATT_KV_HEADS, ATT_HEAD)
    return q, k, v


def attn_mixer(h, hc, wqkv, wo, sink, need_ctx_out):
    B, T, _ = h.shape
    L = hc.shape[1]
    ROWS = T // GRID_W
    rows = jnp.repeat(jnp.arange(ROWS), GRID_W)
    cols = jnp.tile(jnp.arange(GRID_W), ROWS)
    scale = ATT_HEAD ** -0.5
    sink_l = sink.reshape(ATT_KV_HEADS, ATT_GROUP).astype(jnp.float32)

    q, k, v = qkv_split(h, wqkv)
    q = axial_rope(q, rows, cols) * scale
    k = axial_rope(k, rows, cols)
    qc, kc, vc = qkv_split(hc, wqkv)

    nb = T // BLOCK
    span = BLOCK + 2 * WINDOW
    qb = q.reshape(B, nb, BLOCK, ATT_KV_HEADS, ATT_GROUP, ATT_HEAD)
    pad = ((0, 0), (WINDOW, WINDOW), (0, 0), (0, 0))
    idx = jnp.arange(nb)[:, None] * BLOCK + jnp.arange(span)[None, :]
    kw = jnp.pad(k, pad)[:, idx]
    vw = jnp.pad(v, pad)[:, idx]
    kpos = idx - WINDOW
    qpos = jnp.arange(nb)[:, None] * BLOCK + jnp.arange(BLOCK)[None, :]
    valid = ((kpos[:, None, :] >= 0) & (kpos[:, None, :] < T)
             & (jnp.abs(qpos[:, :, None] - kpos[:, None, :]) <= WINDOW))

    s_loc = jnp.einsum('bnqhgd,bnkhd->bnhgqk', qb, kw).astype(jnp.float32)
    s_loc = jnp.where(valid[None, :, None, None], s_loc, NEG)
    s_ctx = jnp.einsum('bnqhgd,bchd->bnhgqc', qb, kc).astype(jnp.float32)
    s_sink = jnp.broadcast_to(sink_l[None, None, :, :, None, None], s_ctx.shape[:-1] + (1,))
    p = jax.nn.softmax(jnp.concatenate([s_sink, s_ctx, s_loc], axis=-1), axis=-1).astype(v.dtype)
    o = (jnp.einsum('bnhgqc,bchd->bnqhgd', p[..., 1:1 + L], vc)
         + jnp.einsum('bnhgqk,bnkhd->bnqhgd', p[..., 1 + L:], vw))
    out = o.reshape(B, T, ATT_Q_HEADS * ATT_HEAD) @ wo

    out_c = None
    if need_ctx_out:
        sc = jnp.einsum('bqhgd,bchd->bhgqc', qc * scale, kc).astype(jnp.float32)
        sc_sink = jnp.broadcast_to(sink_l[None, :, :, None, None], sc.shape[:-1] + (1,))
        pc = jax.nn.softmax(jnp.concatenate([sc_sink, sc], axis=-1), axis=-1)[..., 1:].astype(vc.dtype)
        oc = jnp.einsum('bhgqc,bchd->bqhgd', pc, vc)
        out_c = oc.reshape(B, L, ATT_Q_HEADS * ATT_HEAD) @ wo
    return out, out_c


def ec_moe(h, router_w, w_gate, w_up, w_down):
    B, T, D = h.shape
    cap = EC_CAPACITY * T // N_EXPERTS
    aff = jax.nn.softmax((h @ router_w).astype(jnp.float32), axis=-1)
    gates, idx = lax.top_k(jnp.swapaxes(aff, 1, 2), cap)
    xin = jax.vmap(lambda hb, ib: hb[ib])(h, idx)
    a = jnp.einsum('becd,edf->becf', xin, w_gate)
    u = jnp.einsum('becd,edf->becf', xin, w_up)
    y = jnp.einsum('becf,efd->becd', jax.nn.silu(a) * u, w_down)
    y = y * gates[..., None].astype(y.dtype)
    return jax.vmap(lambda yb, ib: jnp.zeros((T, D), yb.dtype).at[ib.reshape(-1)].add(
        yb.reshape(-1, D)))(y, idx)


def setup_inputs(seed: int = 0) -> dict:
    key = jax.random.key(seed)
    ks = iter(jax.random.split(key, 48))
    nrm = lambda shape, s: jax.random.normal(next(ks), shape, jnp.float32) * s
    D, E, F = D_MODEL, N_EXPERTS, D_EXPERT
    H, N = RW_HEADS, RW_HEAD
    NR, NA = N_RWKV, N_ATTN
    qkv_w = (ATT_Q_HEADS + 2 * ATT_KV_HEADS) * ATT_HEAD
    return {
        "x": nrm((BATCH, SEQ, D), 1.0),
        "c": nrm((BATCH, D), 1.0),
        "ctx": nrm((BATCH, CTX_LEN, D), 1.0),
        "c_ctx": nrm((D,), 1.0),
        "ada_w": nrm((DEPTH, D, N_MOD * D), D ** -0.5),
        "ada_b": nrm((DEPTH, N_MOD * D), 0.01),
        "ln1_g": 1.0 + nrm((DEPTH, D), 0.02),
        "ln1_b": nrm((DEPTH, D), 0.02),
        "ln2_g": 1.0 + nrm((DEPTH, D), 0.02),
        "ln2_b": nrm((DEPTH, D), 0.02),
        "rw_mu": jax.random.uniform(next(ks), (NR, 6, D), jnp.float32),
        "rw_wr": nrm((NR, D, D), D ** -0.5),
        "rw_wk": nrm((NR, D, D), D ** -0.5),
        "rw_wv": nrm((NR, D, D), D ** -0.5),
        "rw_w0": nrm((NR, 2, D), 1.0),
        "rw_w1": nrm((NR, 2, D, RW_DECAY_LORA), D ** -0.5),
        "rw_w2": nrm((NR, 2, RW_DECAY_LORA, D), 0.1 * RW_DECAY_LORA ** -0.5),
        "rw_a0": nrm((NR, 2, D), 0.5),
        "rw_a1": nrm((NR, 2, D, RW_A_LORA), D ** -0.5),
        "rw_a2": nrm((NR, 2, RW_A_LORA, D), 0.1 * RW_A_LORA ** -0.5),
        "rw_g1": nrm((NR, D, RW_GATE_LORA), D ** -0.5),
        "rw_g2": nrm((NR, RW_GATE_LORA, D), RW_GATE_LORA ** -0.5),
        "rw_kk": 0.85 + nrm((NR, D), 0.1),
        "rw_ka": 1.0 + nrm((NR, D), 0.1),
        "rw_rk": nrm((NR, H, N), 0.1),
        "rw_lnx_g": 1.0 + nrm((NR, D), 0.02),
        "rw_lnx_b": nrm((NR, D), 0.02),
        "rw_wo": nrm((NR, D, D), BETA * D ** -0.5),
        "at_wqkv": nrm((NA, D, qkv_w), D ** -0.5),
        "at_wo": nrm((NA, ATT_Q_HEADS * ATT_HEAD, D), BETA * (ATT_Q_HEADS * ATT_HEAD) ** -0.5),
        "at_sink": nrm((NA, ATT_Q_HEADS), 0.5),
        "router_w": nrm((DEPTH, D, E), D ** -0.5),
        "moe_w_gate": nrm((DEPTH, E, D, F), D ** -0.5),
        "moe_w_up": nrm((DEPTH, E, D, F), D ** -0.5),
        "moe_w_down": nrm((DEPTH, E, F, D), BETA * F ** -0.5),
    }


def reference(x, c, ctx, c_ctx, ada_w, ada_b, ln1_g, ln1_b, ln2_g, ln2_b,
              rw_mu, rw_wr, rw_wk, rw_wv, rw_w0, rw_w1, rw_w2, rw_a0, rw_a1, rw_a2,
              rw_g1, rw_g2, rw_kk, rw_ka, rw_rk, rw_lnx_g, rw_lnx_b, rw_wo,
              at_wqkv, at_wo, at_sink, router_w, moe_w_gate, moe_w_up, moe_w_down):
    B = x.shape[0]
    for i in range(DEPTH):
        last = i == DEPTH - 1
        j = i // N_MIXERS
        mod = (jax.nn.silu(c) @ ada_w[i] + ada_b[i]).reshape(B, N_MOD, 1, D_MODEL)
        mod_c = (jax.nn.silu(c_ctx) @ ada_w[i] + ada_b[i]).reshape(N_MOD, D_MODEL)
        h = modulate(x, mod[:, 0], mod[:, 1])
        hc = modulate(ctx, mod_c[0], mod_c[1])
        if i % N_MIXERS == 0:
            o, oc = rwkv_mixer(h, hc, rw_mu[j], rw_wr[j], rw_wk[j], rw_wv[j],
                               rw_w0[j], rw_w1[j], rw_w2[j], rw_a0[j], rw_a1[j], rw_a2[j],
                               rw_g1[j], rw_g2[j], rw_kk[j], rw_ka[j], rw_rk[j],
                               rw_lnx_g[j], rw_lnx_b[j], rw_wo[j], not last)
        else:
            o, oc = attn_mixer(h, hc, at_wqkv[j], at_wo[j], at_sink[j], not last)
        x = layer_norm(ALPHA * x + mod[:, 2] * o, ln1_g[i], ln1_b[i])
        h = modulate(x, mod[:, 3], mod[:, 4])
        x = layer_norm(ALPHA * x + mod[:, 5] * ec_moe(h, router_w[i], moe_w_gate[i], moe_w_up[i],
                                                      moe_w_down[i]), ln2_g[i], ln2_b[i])
        if not last:
            ctx = layer_norm(ALPHA * ctx + mod_c[2] * oc, ln1_g[i], ln1_b[i])
            hc = modulate(ctx, mod_c[3], mod_c[4])
            ctx = layer_norm(ALPHA * ctx + mod_c[5] * ec_moe(hc, router_w[i], moe_w_gate[i], moe_w_up[i],
                                                             moe_w_down[i]), ln2_g[i], ln2_b[i])
    return x
```

```python
import functools

import jax
import jax.numpy as jnp
from jax import lax
from jax.experimental import pallas as pl
from jax.experimental.pallas import tpu as pltpu

F32 = jnp.float32
BF16 = jnp.bfloat16
I32 = jnp.int32
HIGHEST = lax.Precision.HIGHEST

LANES = 128
LN_EPS = 1e-5
N_MOD = 6
HEAD = 64
RW_LNX_EPS = 64e-5
RW_DECAY_SCALE = 0.606531
ATT_KV_HEADS = 4
ATT_GROUP = 4
WINDOW = 128
GRID_W = 64
ROPE_BASE = 10000.0
NEG = -1e30
N_EXPERTS = 16
EC_CAPACITY = 2
CHUNK = 64
ROW_TILE = 256
VMEM_LIMIT = 56 * 2 ** 20


def _cparams(*sem):
    return pltpu.CompilerParams(dimension_semantics=sem, vmem_limit_bytes=VMEM_LIMIT)


def _dot(a, b):
    return jnp.dot(a, b, preferred_element_type=F32)


def _dot_nt(a, b):
    return lax.dot_general(a, b, (((1,), (1,)), ((), ())), preferred_element_type=F32)


def _split(x):
    hi = x.astype(BF16)
    lo = (x - hi.astype(F32)).astype(BF16)
    return hi, lo


def _head_sum(x):
    d = x.shape[-1]
    gr = (lax.broadcasted_iota(I32, (d, LANES), 0) // HEAD
          == lax.broadcasted_iota(I32, (d, LANES), 1)).astype(BF16)
    ge = (lax.broadcasted_iota(I32, (LANES, d), 0)
          == lax.broadcasted_iota(I32, (LANES, d), 1) // HEAD).astype(BF16)
    hi, lo = _split(x)
    s = _dot(hi, gr) + _dot(lo, gr)
    shi, slo = _split(s)
    return _dot(shi, ge) + _dot(slo, ge)


def _layer_norm(x, g, b):
    mu = jnp.mean(x, axis=-1, keepdims=True)
    xc = x - mu
    var = jnp.mean(xc * xc, axis=-1, keepdims=True)
    return xc * lax.rsqrt(var + LN_EPS) * g + b


def _ada_kernel(cc_ref, w_ref, b_ref, o_ref):
    cc = cc_ref[...]
    s = cc * jax.nn.sigmoid(cc)
    o_ref[0] = jnp.dot(s, w_ref[0], precision=HIGHEST, preferred_element_type=F32) + b_ref[0]


def _ada_mod(cc, ada_w, ada_b):
    depth, d, n = ada_w.shape
    tn = n // 4
    return pl.pallas_call(
        _ada_kernel,
        grid=(depth, n // tn),
        in_specs=[pl.BlockSpec((8, d), lambda l, j: (0, 0)),
                  pl.BlockSpec((1, d, tn), lambda l, j: (l, 0, j)),
                  pl.BlockSpec((1, 1, tn), lambda l, j: (l, 0, j))],
        out_specs=pl.BlockSpec((1, 8, tn), lambda l, j: (l, 0, j)),
        out_shape=jax.ShapeDtypeStruct((depth, 8, n), F32),
        compiler_params=_cparams("arbitrary", "arbitrary"),
    )(cc, ada_w, ada_b.reshape(depth, 1, n))


def _rwkv_proj_kernel(nctx_blk, x_ref, xp_ref, xn_ref, mod_ref, vec_ref, wr_ref, wk_ref, wv_ref,
                      w1_ref, a1_ref, g1_ref, w2_ref, a2_ref, g2_ref,
                      r_ref, v_ref, kk_ref, g_ref, bonus_ref,
                      lw0_ref, k0_ref, e0_ref, lw1_ref, k1_ref, e1_ref):
    i = pl.program_id(1)
    nblk = pl.num_programs(1)
    tm = x_ref.shape[1]
    m = mod_ref[0]
    shift, scale = m[0:1], m[1:2]
    h = x_ref[0] * (1 + scale) + shift
    first = (i == 0) | (i == nctx_blk)
    last = (i == nctx_blk - 1) | (i == nblk - 1)
    hp = jnp.where(first, 0.0, xp_ref[0][7:8] * (1 + scale) + shift)
    hn = jnp.where(last, 0.0, xn_ref[0][0:1] * (1 + scale) + shift)
    rows = lax.broadcasted_iota(I32, (tm, 1), 0)
    prev = jnp.where(rows == 0, hp, pltpu.roll(h, 1, 0))
    nxt = jnp.where(rows == tm - 1, hn, pltpu.roll(h, tm - 1, 0))
    xx = 0.5 * (prev + nxt) - h
    vec = vec_ref[...]
    mix = lambda j: (h + xx * vec[j:j + 1]).astype(BF16)
    r = _dot(mix(0), wr_ref[...])
    k = _dot(mix(2), wk_ref[...])
    v = _dot(mix(3), wv_ref[...])
    tw = jnp.tanh(_dot(mix(1), w1_ref[...])).astype(BF16)
    ta = _dot(mix(4), a1_ref[...]).astype(BF16)
    g = _dot(jax.nn.sigmoid(_dot(mix(5), g1_ref[...])).astype(BF16), g2_ref[...])
    kk = k * vec[10:11]
    nrm = jnp.sqrt(_head_sum(kk * kk))
    kk = kk / jnp.maximum(nrm, 1e-12)
    bonus = _head_sum(r * k * vec[12:13]) * v
    r_ref[0] = r
    v_ref[0] = v
    kk_ref[0] = kk
    g_ref[0] = g
    bonus_ref[0] = bonus
    for d, (lw_ref, kd_ref, e_ref) in enumerate(((lw0_ref, k0_ref, e0_ref), (lw1_ref, k1_ref, e1_ref))):
        lw_ref[0] = -RW_DECAY_SCALE * jax.nn.sigmoid(vec[6 + d:7 + d] + _dot(tw, w2_ref[d]))
        eta = jax.nn.sigmoid(vec[8 + d:9 + d] + _dot(ta, a2_ref[d]))
        e_ref[0] = eta
        kd_ref[0] = k * (1 + (eta - 1) * vec[11:12])


def _rwkv_proj(xa, mods, nctx_blk, vec, wr, wk, wv, w1, a1, g1, w2, a2, g2):
    b, r, d = xa.shape
    tm = ROW_TILE
    nblk = r // tm
    hb = tm // 8
    row = pl.BlockSpec((1, tm, d), lambda bb, i: (bb, i, 0))
    full = lambda a: pl.BlockSpec(a.shape, lambda bb, i: (0,) * a.ndim)
    out = jax.ShapeDtypeStruct((b, r, d), F32)
    return pl.pallas_call(
        functools.partial(_rwkv_proj_kernel, nctx_blk),
        grid=(b, nblk),
        in_specs=[row,
                  pl.BlockSpec((1, 8, d), lambda bb, i: (bb, jnp.maximum(i * hb - 1, 0), 0)),
                  pl.BlockSpec((1, 8, d), lambda bb, i: (bb, jnp.minimum((i + 1) * hb, r // 8 - 1), 0)),
                  pl.BlockSpec((1, N_MOD, d), lambda bb, i: (bb * 2 + (i >= nctx_blk).astype(I32), 0, 0)),
                  full(vec), full(wr), full(wk), full(wv), full(w1), full(a1), full(g1),
                  full(w2), full(a2), full(g2)],
        out_specs=[row] * 11,
        out_shape=[out] * 11,
        compiler_params=_cparams("parallel", "arbitrary"),
    )(xa, xa, xa, mods, vec, wr, wk, wv, w1, a1, g1, w2, a2, g2)


def _scan_kernel(rf_ref, vf_ref, kkf_ref, lw0_ref, k0_ref, e0_ref,
                 rr_ref, vr_ref, kkr_ref, lw1_ref, k1_ref, e1_ref,
                 yf_ref, yr_ref, s_ref):
    c = CHUNK

    @pl.when(pl.program_id(1) == 0)
    def _():
        s_ref[...] = jnp.zeros_like(s_ref)

    ti = lax.broadcasted_iota(I32, (c, c), 0)
    tj = lax.broadcasted_iota(I32, (c, c), 1)
    rr = lax.broadcasted_iota(I32, (2 * c, 2 * c), 0)
    cc = lax.broadcasted_iota(I32, (2 * c, 2 * c), 1)
    same = (rr // c) == (cc // c)
    tt, ss = rr % c, cc % c
    lane = lax.broadcasted_iota(I32, (1, 2 * HEAD), 1)
    m0 = lane < HEAD

    dirs = ((rf_ref, vf_ref, kkf_ref, lw0_ref, k0_ref, e0_ref, yf_ref),
            (rr_ref, vr_ref, kkr_ref, lw1_ref, k1_ref, e1_ref, yr_ref))
    for d, (r_ref, v_ref, kk_ref, lw_ref, kd_ref, e_ref, y_ref) in enumerate(dirs):
        if d == 0:
            tri = (tj <= ti).astype(F32)
            strict = same & (ss < tt)
            incl = same & (ss <= tt)
        else:
            tri = (tj >= ti).astype(F32)
            strict = same & (ss > tt)
            incl = same & (ss >= tt)
        lw = lw_ref[0]
        cs = jnp.dot(tri, lw, precision=HIGHEST, preferred_element_type=F32)
        gam = jnp.exp(cs)
        gam_inv = jnp.exp(-cs)
        kk = kk_ref[0]
        a_t = -(kk * jnp.exp(cs - lw))
        b_t = kk * e_ref[0] * gam_inv
        k_t = kd_ref[0] * gam_inv
        r_t = r_ref[0] * gam
        v_all = v_ref[0]
        g_end = gam[c - 1:c] if d == 0 else gam[0:1]
        for p in range(lw.shape[1] // (2 * HEAD)):
            sl = slice(2 * HEAD * p, 2 * HEAD * (p + 1))

            def stack2(x):
                xs = x[:, sl]
                return jnp.concatenate([jnp.where(m0, xs, 0.0), jnp.where(m0, 0.0, xs)], axis=0)

            a2, r2, b2, k2, v2 = stack2(a_t), stack2(r_t), stack2(b_t), stack2(k_t), stack2(v_all)
            ar = jnp.concatenate([a2, r2], axis=0).astype(BF16)
            bk = jnp.concatenate([b2, k2], axis=0)
            m1 = _dot_nt(ar, bk.astype(BF16))
            lab = jnp.where(strict, m1[:2 * c, :2 * c], 0.0)
            lak = jnp.where(strict, m1[:2 * c, 2 * c:], 0.0)
            lr = jnp.concatenate([jnp.where(incl, m1[2 * c:, :2 * c], 0.0),
                                  jnp.where(incl, m1[2 * c:, 2 * c:], 0.0)], axis=1)
            s0 = s_ref[d, p]
            ars = _dot_nt(ar, s0.astype(BF16))
            v2b = v2.astype(BF16)
            x = ars[:2 * c] + _dot(lak.astype(BF16), v2b)
            pw = lab
            for it in range(6):
                pb = pw.astype(BF16)
                x = x + _dot(pb, x.astype(BF16))
                if it < 5:
                    pw = _dot(pb, pb)
            uv = jnp.concatenate([x, v2], axis=0)
            y2 = ars[2 * c:] + _dot(lr.astype(BF16), uv.astype(BF16))
            y_ref[0, :, sl] = y2[:c] + y2[c:]
            ge = g_end[:, sl]
            s_ref[d, p] = s0 * ge + _dot(uv.T.astype(BF16), (bk * ge).astype(BF16))


def _rwkv_scan(r, v, kk, lw0, k0, e0, lw1, k1, e1, nctx_chunks):
    b, rows, d = r.shape
    c = CHUNK
    nc = rows // c
    fwd = lambda bb, i: (bb, i, 0)
    rev = lambda bb, i: (bb, jnp.where(i < nctx_chunks, nctx_chunks - 1 - i, nc - 1 - (i - nctx_chunks)), 0)
    sf = pl.BlockSpec((1, c, d), fwd)
    sr = pl.BlockSpec((1, c, d), rev)
    out = jax.ShapeDtypeStruct((b, rows, d), F32)
    return pl.pallas_call(
        _scan_kernel,
        grid=(b, nc),
        in_specs=[sf] * 6 + [sr] * 6,
        out_specs=[sf, sr],
        out_shape=[out, out],
        scratch_shapes=[pltpu.VMEM((2, d // (2 * HEAD), 2 * HEAD, 2 * HEAD), F32)],
        compiler_params=_cparams("parallel", "arbitrary"),
    )(r, v, kk, lw0, k0, e0, r, v, kk, lw1, k1, e1)


def _post_mixer(z, x, m, wo_ref, lnv):
    o = _dot(z.astype(BF16), wo_ref[...])
    x1 = _layer_norm(lnv[2:3] * x + m[2:3] * o, lnv[0:1], lnv[1:2])
    return x1, x1 * (1 + m[4:5]) + m[3:4]


def _rwkv_out_kernel(yf_ref, yr_ref, bonus_ref, g_ref, x_ref, mod_ref, lnx_ref, wo_ref, lnv_ref,
                     x1_ref, h2_ref):
    y = yf_ref[0] + yr_ref[0]
    mu = _head_sum(y) * (1.0 / HEAD)
    yc = y - mu
    var = _head_sum(yc * yc) * (1.0 / HEAD)
    lnx = lnx_ref[...]
    yn = yc * lax.rsqrt(var + RW_LNX_EPS) * lnx[0:1] + lnx[1:2]
    z = (yn + bonus_ref[0]) * g_ref[0]
    x1, h2 = _post_mixer(z, x_ref[0], mod_ref[0], wo_ref, lnv_ref[...])
    x1_ref[0] = x1
    h2_ref[0] = h2


def _rwkv_out(yf, yr, bonus, g, xa, mods, nctx_blk, lnx, wo, lnv):
    b, r, d = xa.shape
    tm = ROW_TILE
    row = pl.BlockSpec((1, tm, d), lambda bb, i: (bb, i, 0))
    full = lambda a: pl.BlockSpec(a.shape, lambda bb, i: (0,) * a.ndim)
    out = jax.ShapeDtypeStruct((b, r, d), F32)
    return pl.pallas_call(
        _rwkv_out_kernel,
        grid=(b, r // tm),
        in_specs=[row] * 5 + [pl.BlockSpec((1, N_MOD, d), lambda bb, i: (bb * 2 + (i >= nctx_blk).astype(I32), 0, 0)),
                              full(lnx), full(wo), full(lnv)],
        out_specs=[row, row],
        out_shape=[out, out],
        compiler_params=_cparams("parallel", "arbitrary"),
    )(yf, yr, bonus, g, xa, mods, lnx, wo, lnv)


def _attn_out_kernel(o_ref, x_ref, mod_ref, wo_ref, lnv_ref, x1_ref, h2_ref):
    x1, h2 = _post_mixer(o_ref[0], x_ref[0], mod_ref[0], wo_ref, lnv_ref[...])
    x1_ref[0] = x1
    h2_ref[0] = h2


def _attn_out(o, xa, mods, blk_off, wo, lnv):
    b, t, d = o.shape
    tm = ROW_TILE
    row = pl.BlockSpec((1, tm, d), lambda bb, i: (bb, i, 0))
    full = lambda a: pl.BlockSpec(a.shape, lambda bb, i: (0,) * a.ndim)
    out = jax.ShapeDtypeStruct((b, t, d), F32)
    return pl.pallas_call(
        _attn_out_kernel,
        grid=(b, t // tm),
        in_specs=[row, pl.BlockSpec((1, tm, d), lambda bb, i: (bb, i + blk_off, 0)),
                  pl.BlockSpec((1, N_MOD, d), lambda bb, i: (bb * 2 + 1, 0, 0)),
                  full(wo), full(lnv)],
        out_specs=[row, row],
        out_shape=[out, out],
        compiler_params=_cparams("parallel", "arbitrary"),
    )(o, xa, mods, wo, lnv)


def _router_kernel(h_ref, rw_ref, aff_ref):
    logits = jnp.dot(h_ref[0], rw_ref[...], precision=HIGHEST, preferred_element_type=F32)
    lane = lax.broadcasted_iota(I32, logits.shape, 1)
    logits = jnp.where(lane < N_EXPERTS, logits, NEG)
    mx = jnp.max(logits, axis=-1, keepdims=True)
    ex = jnp.exp(logits - mx)
    aff_ref[0] = ex / jnp.sum(ex, axis=-1, keepdims=True)


def _router(h2, rw_pad):
    b, r, d = h2.shape
    tm = ROW_TILE
    return pl.pallas_call(
        _router_kernel,
        grid=(b, r // tm),
        in_specs=[pl.BlockSpec((1, tm, d), lambda bb, i: (bb, i, 0)),
                  pl.BlockSpec(rw_pad.shape, lambda bb, i: (0, 0))],
        out_specs=pl.BlockSpec((1, tm, LANES), lambda bb, i: (bb, i, 0)),
        out_shape=jax.ShapeDtypeStruct((b, r, LANES), F32),
        compiler_params=_cparams("parallel", "arbitrary"),
    )(h2, rw_pad)


def _select_kernel(cap, aff_ref, idx_ref, gate_ref, c_ref, sa_ref):
    tn = aff_ref.shape[1]
    blk = LANES
    keys = lambda: lax.bitcast_convert_type(aff_ref[0], I32)

    def bit_step(j, thr):
        cand = thr | jnp.left_shift(jnp.int32(1), 30 - j)
        cnt = jnp.sum((keys() >= cand).astype(F32), axis=0, keepdims=True)
        return jnp.where(cnt >= cap, cand, thr)

    thr = lax.fori_loop(0, 31, bit_step, jnp.zeros((1, LANES), I32))
    need = cap - jnp.sum((keys() > thr).astype(F32), axis=0, keepdims=True)
    tril = (lax.broadcasted_iota(I32, (blk, blk), 1) <= lax.broadcasted_iota(I32, (blk, blk), 0)).astype(BF16)

    def blk_step(j, carry):
        tie_off, sel_off = carry
        rows = pl.ds(pl.multiple_of(j * blk, blk), blk)
        a = aff_ref[0, rows, :]
        kb = lax.bitcast_convert_type(a, I32)
        eq = (kb == thr).astype(F32)
        tie_rank = _dot(tril, eq.astype(BF16)) + tie_off - eq
        sel = ((kb > thr) | ((eq > 0) & (tie_rank < need))).astype(F32)
        cinc = _dot(tril, sel.astype(BF16)) + sel_off
        c_ref[rows, :] = cinc
        sa_ref[rows, :] = sel * a
        return (tie_off + jnp.sum(eq, axis=0, keepdims=True), sel_off + jnp.sum(sel, axis=0, keepdims=True))

    zero = jnp.zeros((1, LANES), F32)
    lax.fori_loop(0, tn // blk, blk_step, (zero, zero))

    tc = min(tn, 256)
    slot = lax.broadcasted_iota(I32, (1, cap), 1).astype(F32)
    for e in range(N_EXPERTS):
        def chunk(j, carry):
            ai, ag = carry
            rows = pl.ds(pl.multiple_of(j * tc, tc), tc)
            ce = c_ref[rows, e:e + 1]
            se = sa_ref[rows, e:e + 1]
            ai = ai + jnp.sum((ce <= slot).astype(F32), axis=0, keepdims=True)
            ag = ag + jnp.sum(jnp.where(ce == slot + 1.0, se, 0.0), axis=0, keepdims=True)
            return ai, ag

        zc = jnp.zeros((1, cap), F32)
        ai, ag = lax.fori_loop(0, tn // tc, chunk, (zc, zc))
        idx_ref[0, e:e + 1, :] = ai.astype(I32)
        gate_ref[0, e:e + 1, :] = ag


def _select(aff, cap):
    b, tn, _ = aff.shape
    return pl.pallas_call(
        functools.partial(_select_kernel, cap),
        grid=(b,),
        in_specs=[pl.BlockSpec((1, tn, LANES), lambda bb: (bb, 0, 0))],
        out_specs=[pl.BlockSpec((1, N_EXPERTS, cap), lambda bb: (bb, 0, 0))] * 2,
        out_shape=[jax.ShapeDtypeStruct((b, N_EXPERTS, cap), I32),
                   jax.ShapeDtypeStruct((b, N_EXPERTS, cap), F32)],
        scratch_shapes=[pltpu.VMEM((tn, LANES), F32), pltpu.VMEM((tn, LANES), F32)],
        compiler_params=_cparams("parallel"),
    )(aff)


def _expert_kernel(idx_ref, h_hbm, gate_ref, wg_ref, wu_ref, wd_ref, acc_in, acc_hbm,
                   stage, xin, yacc, sem):
    del acc_in
    e = pl.program_id(0)
    f = pl.program_id(1)
    m = stage.shape[0]
    base = e * m

    def gather(src):
        def issue(i, carry):
            pltpu.make_async_copy(src.at[pl.ds(idx_ref[base + i], 1)], stage.at[pl.ds(i, 1)], sem).start()
            return carry

        def wait(i, carry):
            pltpu.make_async_copy(src.at[pl.ds(0, 1)], stage.at[pl.ds(i, 1)], sem).wait()
            return carry

        lax.fori_loop(0, m, issue, 0)
        lax.fori_loop(0, m, wait, 0)

    @pl.when(f == 0)
    def _():
        gather(h_hbm)
        xin[...] = stage[...].astype(BF16)

    x = xin[...]
    a = _dot(x, wg_ref[0].astype(BF16))
    u = _dot(x, wu_ref[0].astype(BF16))
    hm = (a * jax.nn.sigmoid(a) * u).astype(BF16)
    part = _dot(hm, wd_ref[0].astype(BF16))

    @pl.when(f == 0)
    def _():
        yacc[...] = part

    @pl.when(f > 0)
    def _():
        yacc[...] += part

    @pl.when(f == pl.num_programs(1) - 1)
    def _():
        gather(acc_hbm)
        stage[...] = stage[...] + gate_ref[0] * yacc[...]

        def issue(i, carry):
            pltpu.make_async_copy(stage.at[pl.ds(i, 1)], acc_hbm.at[pl.ds(idx_ref[base + i], 1)], sem).start()
            return carry

        def wait(i, carry):
            pltpu.make_async_copy(stage.at[pl.ds(i, 1)], acc_hbm.at[pl.ds(0, 1)], sem).wait()
            return carry

        lax.fori_loop(0, m, issue, 0)
        lax.fori_loop(0, m, wait, 0)


def _experts(idx_flat, gates, h_rows, acc, w_gate, w_up, w_down):
    n_e, d, f = w_gate.shape
    m = gates.shape[1]
    tf = 512
    grid_spec = pltpu.PrefetchScalarGridSpec(
        num_scalar_prefetch=1,
        grid=(n_e, f // tf),
        in_specs=[pl.BlockSpec(memory_space=pl.ANY),
                  pl.BlockSpec((1, m, 1), lambda e, j, idx: (e, 0, 0)),
                  pl.BlockSpec((1, d, tf), lambda e, j, idx: (e, 0, j)),
                  pl.BlockSpec((1, d, tf), lambda e, j, idx: (e, 0, j)),
                  pl.BlockSpec((1, tf, d), lambda e, j, idx: (e, j, 0)),
                  pl.BlockSpec(memory_space=pl.ANY)],
        out_specs=pl.BlockSpec(memory_space=pl.ANY),
        scratch_shapes=[pltpu.VMEM((m, d), F32), pltpu.VMEM((m, d), BF16), pltpu.VMEM((m, d), F32),
                        pltpu.SemaphoreType.DMA(())],
    )
    return pl.pallas_call(
        _expert_kernel,
        grid_spec=grid_spec,
        out_shape=jax.ShapeDtypeStruct(acc.shape, F32),
        input_output_aliases={6: 0},
        compiler_params=_cparams("arbitrary", "arbitrary"),
    )(idx_flat, h_rows, gates, w_gate, w_up, w_down, acc)


def _moe_ln_kernel(x_ref, moe_ref, mod_ref, lnv_ref, o_ref):
    lnv = lnv_ref[...]
    m = mod_ref[0]
    o_ref[0] = _layer_norm(lnv[2:3] * x_ref[0] + m[5:6] * moe_ref[0], lnv[0:1], lnv[1:2])


def _moe_ln(x1, moe, mods, nctx_blk, lnv):
    b, r, d = x1.shape
    tm = ROW_TILE
    row = pl.BlockSpec((1, tm, d), lambda bb, i: (bb, i, 0))
    return pl.pallas_call(
        _moe_ln_kernel,
        grid=(b, r // tm),
        in_specs=[row, row,
                  pl.BlockSpec((1, N_MOD, d), lambda bb, i: (bb * 2 + (i >= nctx_blk).astype(I32), 0, 0)),
                  pl.BlockSpec(lnv.shape, lambda bb, i: (0, 0))],
        out_specs=row,
        out_shape=jax.ShapeDtypeStruct((b, r, d), F32),
        compiler_params=_cparams("parallel", "arbitrary"),
    )(x1, moe, mods, lnv)


def _ec_moe(h2, streams, rw_pad, w_gate, w_up, w_down):
    b, r, d = h2.shape
    aff = _router(h2, rw_pad)
    idx_parts, gate_parts = [], []
    for off, tn in streams:
        cap = EC_CAPACITY * tn // N_EXPERTS
        idx, gates = _select(lax.slice_in_dim(aff, off, off + tn, axis=1), cap)
        rows = idx + (jnp.arange(b, dtype=I32) * r + off)[:, None, None]
        idx_parts.append(jnp.swapaxes(rows, 0, 1).reshape(N_EXPERTS, b * cap))
        gate_parts.append(jnp.swapaxes(gates, 0, 1).reshape(N_EXPERTS, b * cap))
    idx_flat = jnp.concatenate(idx_parts, axis=1)
    gates = jnp.concatenate(gate_parts, axis=1)
    acc = _experts(idx_flat.reshape(-1), gates[..., None], h2.reshape(b * r, d),
                   jnp.zeros((b * r, d), F32), w_gate, w_up, w_down)
    return acc.reshape(b, r, d)


def _qkv_kernel(x_ref, mod_ref, cos_ref, sin_ref, wq_ref, wk_ref, wv_ref, q_ref, k_ref, v_ref):
    m = mod_ref[0]
    h = (x_ref[0] * (1 + m[1:2]) + m[0:1]).astype(BF16)
    d = x_ref.shape[2]
    rep = d // LANES
    cos = jnp.concatenate([cos_ref[...]] * rep, axis=1)
    sin = jnp.concatenate([sin_ref[...]] * rep, axis=1)
    lane = lax.broadcasted_iota(I32, (1, d), 1)
    lower = (lane % 32) < 16

    def rope(t):
        partner = jnp.where(lower, pltpu.roll(t, d - 16, 1), pltpu.roll(t, 16, 1))
        return t * cos + partner * sin

    q_ref[0] = (rope(_dot(h, wq_ref[...])) * (HEAD ** -0.5)).astype(BF16)
    k_ref[0] = rope(_dot(h, wk_ref[...])).astype(BF16)
    v_ref[0] = _dot(h, wv_ref[...]).astype(BF16)


def _qkv(xa, mods, nctx_blk, cos, sin, wq, wk4, wv4):
    b, r, d = xa.shape
    tm = ROW_TILE
    row = pl.BlockSpec((1, tm, d), lambda bb, i: (bb, i, 0))
    tab = pl.BlockSpec((tm, LANES), lambda bb, i: (i, 0))
    full = lambda a: pl.BlockSpec(a.shape, lambda bb, i: (0,) * a.ndim)
    out = jax.ShapeDtypeStruct((b, r, d), BF16)
    return pl.pallas_call(
        _qkv_kernel,
        grid=(b, r // tm),
        in_specs=[row, pl.BlockSpec((1, N_MOD, d), lambda bb, i: (bb * 2 + (i >= nctx_blk).astype(I32), 0, 0)),
                  tab, tab, full(wq), full(wk4), full(wv4)],
        out_specs=[row] * 3,
        out_shape=[out] * 3,
        compiler_params=_cparams("parallel", "arbitrary"),
    )(xa, mods, cos, sin, wq, wk4, wv4)


def _attn_kernel(n_ctx, sink_ref, q_ref, kc_ref, vc_ref, kp_ref, k0_ref, kn_ref, vp_ref, v0_ref, vn_ref, o_ref):
    nb = pl.program_id(1)
    n_blk = pl.num_programs(1)
    bq = q_ref.shape[1]
    gw = ATT_GROUP * HEAD
    nk = n_ctx + 3 * bq
    col = lax.broadcasted_iota(I32, (bq, nk), 1)
    qpos = nb * bq + lax.broadcasted_iota(I32, (bq, nk), 0)
    kpos = (nb - 1) * bq + (col - n_ctx)
    valid = (col < n_ctx) | ((kpos >= 0) & (kpos < n_blk * bq) & (jnp.abs(qpos - kpos) <= WINDOW))
    lane = lax.broadcasted_iota(I32, (1, gw), 1)
    for h in range(ATT_KV_HEADS):
        sl = slice(gw * h, gw * (h + 1))
        keys = jnp.concatenate([kc_ref[0, :, sl], kp_ref[0, :, sl], k0_ref[0, :, sl], kn_ref[0, :, sl]], axis=0)
        vals = jnp.concatenate([vc_ref[0, :, sl], vp_ref[0, :, sl], v0_ref[0, :, sl], vn_ref[0, :, sl]], axis=0)
        qh = q_ref[0, :, sl]
        acc = jnp.zeros((bq, gw), F32)
        for g in range(ATT_GROUP):
            mg = (lane // HEAD) == g
            s = _dot_nt(jnp.where(mg, qh, jnp.zeros_like(qh)), keys)
            s = jnp.where(valid, s, NEG)
            sink = sink_ref[h * ATT_GROUP + g]
            mx = jnp.maximum(jnp.max(s, axis=-1, keepdims=True), sink)
            p = jnp.exp(s - mx)
            den = jnp.sum(p, axis=-1, keepdims=True) + jnp.exp(sink - mx)
            o = _dot((p / den).astype(BF16), vals)
            acc = acc + jnp.where(mg, o, 0.0)
        o_ref[0, :, sl] = acc.astype(BF16)


def _attention(q, k4, v4, sink, n_ctx):
    b, r, d = q.shape
    bq = WINDOW
    t = r - n_ctx
    nb = t // bq
    off = n_ctx // bq
    blk = lambda f: pl.BlockSpec((1, bq, d), f)
    ctx_spec = pl.BlockSpec((1, n_ctx, d), lambda bb, i: (bb, 0, 0))
    prev = lambda bb, i: (bb, jnp.maximum(i - 1, 0) + off, 0)
    cur = lambda bb, i: (bb, i + off, 0)
    nxt = lambda bb, i: (bb, jnp.minimum(i + 1, nb - 1) + off, 0)
    return pl.pallas_call(
        functools.partial(_attn_kernel, n_ctx),
        grid=(b, nb),
        in_specs=[pl.BlockSpec(memory_space=pltpu.SMEM), blk(cur), ctx_spec, ctx_spec,
                  blk(prev), blk(cur), blk(nxt), blk(prev), blk(cur), blk(nxt)],
        out_specs=pl.BlockSpec((1, bq, d), lambda bb, i: (bb, i, 0)),
        out_shape=jax.ShapeDtypeStruct((b, t, d), BF16),
        compiler_params=_cparams("parallel", "arbitrary"),
    )(sink, q, k4, v4, k4, k4, k4, v4, v4, v4)


def _rope_tables(n_ctx, t):
    half = HEAD // 2
    inv = 1.0 / (ROPE_BASE ** (jnp.arange(0, half, 2, dtype=F32) / half))
    pos = jnp.arange(t)
    ang_r = (pos // GRID_W).astype(F32)[:, None] * inv
    ang_c = (pos % GRID_W).astype(F32)[:, None] * inv
    cos = jnp.concatenate([jnp.cos(ang_r)] * 2 + [jnp.cos(ang_c)] * 2, axis=1)
    sin = jnp.concatenate([-jnp.sin(ang_r), jnp.sin(ang_r), -jnp.sin(ang_c), jnp.sin(ang_c)], axis=1)
    cos = jnp.concatenate([jnp.ones((n_ctx, HEAD), F32), cos], axis=0)
    sin = jnp.concatenate([jnp.zeros((n_ctx, HEAD), F32), sin], axis=0)
    return jnp.tile(cos, (1, 2)), jnp.tile(sin, (1, 2))


def kernel(x, c, ctx, c_ctx, ada_w, ada_b, ln1_g, ln1_b, ln2_g, ln2_b, rw_mu, rw_wr, rw_wk, rw_wv, rw_w0, rw_w1, rw_w2, rw_a0, rw_a1, rw_a2, rw_g1, rw_g2, rw_kk, rw_ka, rw_rk, rw_lnx_g, rw_lnx_b, rw_wo, at_wqkv, at_wo, at_sink, router_w, moe_w_gate, moe_w_up, moe_w_down):
    b, t, d = x.shape
    n_ctx = ctx.shape[1]
    depth = ada_w.shape[0]
    alpha = (2 * depth) ** 0.25
    assert n_ctx % ROW_TILE == 0 and t % ROW_TILE == 0 and b + 1 <= 8
    nctx_blk = n_ctx // ROW_TILE

    cc = jnp.concatenate([c, c_ctx[None], jnp.zeros((8 - b - 1, d), F32)], axis=0)
    mod_all = _ada_mod(cc, ada_w, ada_b).reshape(depth, 8, N_MOD, d)
    xa = jnp.concatenate([ctx, x], axis=1)
    bf = lambda w: w.astype(BF16)
    pad_rows = lambda w, lo: jnp.pad(w, ((lo, LANES - lo - w.shape[0]), (0, 0)))

    for i in range(depth):
        last = i == depth - 1
        j = i // 2
        mods = jnp.stack([jnp.broadcast_to(mod_all[i, b], (b, N_MOD, d)), mod_all[i, :b]], axis=1).reshape(2 * b, N_MOD, d)
        lnv1 = jnp.stack([ln1_g[i], ln1_b[i], jnp.full((d,), alpha, F32)])
        lnv2 = jnp.stack([ln2_g[i], ln2_b[i], jnp.full((d,), alpha, F32)])
        rw_pad = jnp.pad(router_w[i], ((0, 0), (0, LANES - N_EXPERTS)))
        if i % 2 == 0:
            vec = jnp.concatenate([rw_mu[j], rw_w0[j], rw_a0[j], rw_kk[j][None], rw_ka[j][None],
                                   rw_rk[j].reshape(1, d), jnp.zeros((3, d), F32)], axis=0)
            w1 = bf(jnp.concatenate([rw_w1[j, 0], rw_w1[j, 1]], axis=1))
            a1 = bf(jnp.concatenate([rw_a1[j, 0], rw_a1[j, 1]], axis=1))
            w2 = bf(jnp.stack([pad_rows(rw_w2[j, 0], 0), pad_rows(rw_w2[j, 1], HEAD)]))
            a2 = bf(jnp.stack([pad_rows(rw_a2[j, 0], 0), pad_rows(rw_a2[j, 1], HEAD)]))
            (r, v, kk, g, bonus, lw0, k0, e0, lw1, k1, e1) = _rwkv_proj(
                xa, mods, nctx_blk, vec, bf(rw_wr[j]), bf(rw_wk[j]), bf(rw_wv[j]), w1, a1, bf(rw_g1[j]),
                w2, a2, bf(rw_g2[j]))
            yf, yr = _rwkv_scan(r, v, kk, lw0, k0, e0, lw1, k1, e1, n_ctx // CHUNK)
            lnx = jnp.stack([rw_lnx_g[j], rw_lnx_b[j]])
            x1, h2 = _rwkv_out(yf, yr, bonus, g, xa, mods, nctx_blk, lnx, bf(rw_wo[j]), lnv1)
        else:
            nq = ATT_KV_HEADS * ATT_GROUP * HEAD
            nk = ATT_KV_HEADS * HEAD
            wqkv = at_wqkv[j]
            tile4 = lambda w: jnp.tile(w.reshape(d, ATT_KV_HEADS, 1, HEAD), (1, 1, ATT_GROUP, 1)).reshape(d, nq)
            cos, sin = _rope_tables(n_ctx, t)
            q, k4, v4 = _qkv(xa, mods, nctx_blk, cos, sin, bf(wqkv[:, :nq]),
                             bf(tile4(wqkv[:, nq:nq + nk])), bf(tile4(wqkv[:, nq + nk:])))
            o = _attention(q, k4, v4, at_sink[j], n_ctx)
            x1, h2 = _attn_out(o, xa, mods, nctx_blk, bf(at_wo[j]), lnv1)
        if last:
            if x1.shape[1] != t:
                x1, h2 = x1[:, n_ctx:], h2[:, n_ctx:]
            moe = _ec_moe(h2, [(0, t)], rw_pad, moe_w_gate[i], moe_w_up[i], moe_w_down[i])
            lat_mods = mods.reshape(b, 2, N_MOD, d)[:, 1:2]
            lat_mods = jnp.concatenate([lat_mods, lat_mods], axis=1).reshape(2 * b, N_MOD, d)
            return _moe_ln(x1, moe, lat_mods, 0, lnv2)
        moe = _ec_moe(h2, [(n_ctx, t), (0, n_ctx)], rw_pad, moe_w_gate[i], moe_w_up[i], moe_w_down[i])
        xa = _moe_ln(x1, moe, mods, nctx_blk, lnv2)
    return xa[:, n_ctx:]
```

```python
import functools

import jax
import jax.numpy as jnp
from jax import lax
from jax.experimental import pallas as pl
from jax.experimental.pallas import tpu as pltpu

F32 = jnp.float32
BF16 = jnp.bfloat16
I32 = jnp.int32
HIGHEST = lax.Precision.HIGHEST

LANES = 128
LN_EPS = 1e-5
N_MOD = 6
HEAD = 64
RW_LNX_EPS = 64e-5
RW_DECAY_SCALE = 0.606531
ATT_KV_HEADS = 4
ATT_GROUP = 4
WINDOW = 128
GRID_W = 64
ROPE_BASE = 10000.0
NEG = -1e30
N_EXPERTS = 16
EC_CAPACITY = 2
CHUNK = 64
DMA_UNROLL = 8
ROW_TILE = 256
VMEM_LIMIT = 56 * 2 ** 20


def _cparams(*sem):
    return pltpu.CompilerParams(dimension_semantics=sem, vmem_limit_bytes=VMEM_LIMIT)


def _dot(a, b):
    return jnp.dot(a, b, preferred_element_type=F32)


def _dot_nt(a, b):
    return lax.dot_general(a, b, (((1,), (1,)), ((), ())), preferred_element_type=F32)


def _split(x):
    hi = x.astype(BF16)
    lo = (x - hi.astype(F32)).astype(BF16)
    return hi, lo


def _head_sum(x):
    d = x.shape[-1]
    gr = (lax.broadcasted_iota(I32, (d, LANES), 0) // HEAD
          == lax.broadcasted_iota(I32, (d, LANES), 1)).astype(BF16)
    ge = (lax.broadcasted_iota(I32, (LANES, d), 0)
          == lax.broadcasted_iota(I32, (LANES, d), 1) // HEAD).astype(BF16)
    hi, lo = _split(x)
    s = _dot(hi, gr) + _dot(lo, gr)
    shi, slo = _split(s)
    return _dot(shi, ge) + _dot(slo, ge)


def _layer_norm(x, g, b):
    mu = jnp.mean(x, axis=-1, keepdims=True)
    xc = x - mu
    var = jnp.mean(xc * xc, axis=-1, keepdims=True)
    return xc * lax.rsqrt(var + LN_EPS) * g + b


def _ada_kernel(cc_ref, w_ref, b_ref, o_ref):
    cc = cc_ref[...]
    s = cc * jax.nn.sigmoid(cc)
    o_ref[0] = jnp.dot(s, w_ref[0], precision=HIGHEST, preferred_element_type=F32) + b_ref[0]


def _ada_mod(cc, ada_w, ada_b):
    depth, d, n = ada_w.shape
    tn = n // 4
    return pl.pallas_call(
        _ada_kernel,
        grid=(depth, n // tn),
        in_specs=[pl.BlockSpec((8, d), lambda l, j: (0, 0)),
                  pl.BlockSpec((1, d, tn), lambda l, j: (l, 0, j)),
                  pl.BlockSpec((1, 1, tn), lambda l, j: (l, 0, j))],
        out_specs=pl.BlockSpec((1, 8, tn), lambda l, j: (l, 0, j)),
        out_shape=jax.ShapeDtypeStruct((depth, 8, n), F32),
        compiler_params=_cparams("arbitrary", "arbitrary"),
    )(cc, ada_w, ada_b.reshape(depth, 1, n))


def _rwkv_proj_kernel(nctx_blk, x_ref, xp_ref, xn_ref, mod_ref, vec_ref, wr_ref, wk_ref, wv_ref,
                      w1_ref, a1_ref, g1_ref, w2_ref, a2_ref, g2_ref,
                      r_ref, v_ref, kk_ref, g_ref, bonus_ref,
                      lw0_ref, k0_ref, e0_ref, lw1_ref, k1_ref, e1_ref):
    i = pl.program_id(1)
    nblk = pl.num_programs(1)
    tm = x_ref.shape[1]
    m = mod_ref[0]
    shift, scale = m[0:1], m[1:2]
    h = x_ref[0] * (1 + scale) + shift
    first = (i == 0) | (i == nctx_blk)
    last = (i == nctx_blk - 1) | (i == nblk - 1)
    hp = jnp.where(first, 0.0, xp_ref[0][7:8] * (1 + scale) + shift)
    hn = jnp.where(last, 0.0, xn_ref[0][0:1] * (1 + scale) + shift)
    rows = lax.broadcasted_iota(I32, (tm, 1), 0)
    prev = jnp.where(rows == 0, hp, pltpu.roll(h, 1, 0))
    nxt = jnp.where(rows == tm - 1, hn, pltpu.roll(h, tm - 1, 0))
    xx = 0.5 * (prev + nxt) - h
    vec = vec_ref[...]
    mix = lambda j: (h + xx * vec[j:j + 1]).astype(BF16)
    r = _dot(mix(0), wr_ref[...])
    k = _dot(mix(2), wk_ref[...])
    v = _dot(mix(3), wv_ref[...])
    tw = jnp.tanh(_dot(mix(1), w1_ref[...])).astype(BF16)
    ta = _dot(mix(4), a1_ref[...]).astype(BF16)
    g = _dot(jax.nn.sigmoid(_dot(mix(5), g1_ref[...])).astype(BF16), g2_ref[...])
    kk = k * vec[10:11]
    nrm = jnp.sqrt(_head_sum(kk * kk))
    kk = kk / jnp.maximum(nrm, 1e-12)
    bonus = _head_sum(r * k * vec[12:13]) * v
    r_ref[0] = r
    v_ref[0] = v
    kk_ref[0] = kk
    g_ref[0] = g
    bonus_ref[0] = bonus
    for d, (lw_ref, kd_ref, e_ref) in enumerate(((lw0_ref, k0_ref, e0_ref), (lw1_ref, k1_ref, e1_ref))):
        lw_ref[0] = -RW_DECAY_SCALE * jax.nn.sigmoid(vec[6 + d:7 + d] + _dot(tw, w2_ref[d]))
        eta = jax.nn.sigmoid(vec[8 + d:9 + d] + _dot(ta, a2_ref[d]))
        e_ref[0] = eta
        kd_ref[0] = k * (1 + (eta - 1) * vec[11:12])


def _rwkv_proj(xa, mods, nctx_blk, vec, wr, wk, wv, w1, a1, g1, w2, a2, g2):
    b, r, d = xa.shape
    tm = ROW_TILE
    nblk = r // tm
    hb = tm // 8
    row = pl.BlockSpec((1, tm, d), lambda bb, i: (bb, i, 0))
    full = lambda a: pl.BlockSpec(a.shape, lambda bb, i: (0,) * a.ndim)
    out = jax.ShapeDtypeStruct((b, r, d), F32)
    return pl.pallas_call(
        functools.partial(_rwkv_proj_kernel, nctx_blk),
        grid=(b, nblk),
        in_specs=[row,
                  pl.BlockSpec((1, 8, d), lambda bb, i: (bb, jnp.maximum(i * hb - 1, 0), 0)),
                  pl.BlockSpec((1, 8, d), lambda bb, i: (bb, jnp.minimum((i + 1) * hb, r // 8 - 1), 0)),
                  pl.BlockSpec((1, N_MOD, d), lambda bb, i: (bb * 2 + (i >= nctx_blk).astype(I32), 0, 0)),
                  full(vec), full(wr), full(wk), full(wv), full(w1), full(a1), full(g1),
                  full(w2), full(a2), full(g2)],
        out_specs=[row] * 11,
        out_shape=[out] * 11,
        compiler_params=_cparams("parallel", "arbitrary"),
    )(xa, xa, xa, mods, vec, wr, wk, wv, w1, a1, g1, w2, a2, g2)


def _scan_kernel(rf_ref, vf_ref, kkf_ref, lw0_ref, k0_ref, e0_ref,
                 rr_ref, vr_ref, kkr_ref, lw1_ref, k1_ref, e1_ref,
                 yf_ref, yr_ref, s_ref):
    c = CHUNK

    @pl.when(pl.program_id(1) == 0)
    def _():
        s_ref[...] = jnp.zeros_like(s_ref)

    ti = lax.broadcasted_iota(I32, (c, c), 0)
    tj = lax.broadcasted_iota(I32, (c, c), 1)
    rr = lax.broadcasted_iota(I32, (2 * c, 2 * c), 0)
    cc = lax.broadcasted_iota(I32, (2 * c, 2 * c), 1)
    same = (rr // c) == (cc // c)
    tt, ss = rr % c, cc % c
    lane = lax.broadcasted_iota(I32, (1, 2 * HEAD), 1)
    m0 = lane < HEAD

    dirs = ((rf_ref, vf_ref, kkf_ref, lw0_ref, k0_ref, e0_ref, yf_ref),
            (rr_ref, vr_ref, kkr_ref, lw1_ref, k1_ref, e1_ref, yr_ref))
    n_pair = s_ref.shape[1]
    ch = []
    for d, (r_ref, v_ref, kk_ref, lw_ref, kd_ref, e_ref, y_ref) in enumerate(dirs):
        if d == 0:
            tri = (tj <= ti).astype(F32)
            strict = same & (ss < tt)
            incl = same & (ss <= tt)
        else:
            tri = (tj >= ti).astype(F32)
            strict = same & (ss > tt)
            incl = same & (ss >= tt)
        lw = lw_ref[0]
        cs = jnp.dot(tri, lw, precision=HIGHEST, preferred_element_type=F32)
        gam = jnp.exp(cs)
        gam_inv = jnp.exp(-cs)
        kk = kk_ref[0]
        a_t = -(kk * jnp.exp(cs - lw))
        b_t = kk * e_ref[0] * gam_inv
        k_t = kd_ref[0] * gam_inv
        r_t = r_ref[0] * gam
        v_all = v_ref[0]
        g_end = gam[c - 1:c] if d == 0 else gam[0:1]
        for p in range(n_pair):
            sl = slice(2 * HEAD * p, 2 * HEAD * (p + 1))

            def stack2(x):
                xs = x[:, sl]
                return jnp.concatenate([jnp.where(m0, xs, 0.0), jnp.where(m0, 0.0, xs)], axis=0)

            v2 = stack2(v_all)
            ch.append(dict(
                d=d, p=p, sl=sl, y_ref=y_ref, strict=strict, incl=incl, v2=v2, v2b=v2.astype(BF16),
                ar=jnp.concatenate([stack2(a_t), stack2(r_t)], axis=0).astype(BF16),
                bk=jnp.concatenate([stack2(b_t), stack2(k_t)], axis=0),
                ge=g_end[:, sl], s0=s_ref[d, p]))
    for q in ch:
        q["m1"] = _dot_nt(q["ar"], q["bk"].astype(BF16))
    for q in ch:
        q["ars"] = _dot_nt(q["ar"], q["s0"].astype(BF16))
    for q in ch:
        m1 = q.pop("m1")
        q["pw"] = jnp.where(q["strict"], m1[:2 * c, :2 * c], 0.0)
        lak = jnp.where(q["strict"], m1[:2 * c, 2 * c:], 0.0).astype(BF16)
        q["lr"] = jnp.concatenate([jnp.where(q["incl"], m1[2 * c:, :2 * c], 0.0),
                                   jnp.where(q["incl"], m1[2 * c:, 2 * c:], 0.0)], axis=1).astype(BF16)
        q["x"] = q["ars"][:2 * c] + _dot(lak, q["v2b"])
    for it in range(6):
        for q in ch:
            pb = q["pw"].astype(BF16)
            q["x"] = q["x"] + _dot(pb, q["x"].astype(BF16))
            if it < 5:
                q["pw"] = _dot(pb, pb)
    for q in ch:
        q["uv"] = jnp.concatenate([q["x"], q["v2"]], axis=0)
        y2 = q["ars"][2 * c:] + _dot(q["lr"], q["uv"].astype(BF16))
        q["y_ref"][0, :, q["sl"]] = y2[:c] + y2[c:]
    for q in ch:
        s_ref[q["d"], q["p"]] = (q["s0"] * q["ge"]
                                 + _dot(q["uv"].T.astype(BF16), (q["bk"] * q["ge"]).astype(BF16)))


def _rwkv_scan(r, v, kk, lw0, k0, e0, lw1, k1, e1, nctx_chunks):
    b, rows, d = r.shape
    c = CHUNK
    nc = rows // c
    fwd = lambda bb, i: (bb, i, 0)
    rev = lambda bb, i: (bb, jnp.where(i < nctx_chunks, nctx_chunks - 1 - i, nc - 1 - (i - nctx_chunks)), 0)
    sf = pl.BlockSpec((1, c, d), fwd)
    sr = pl.BlockSpec((1, c, d), rev)
    out = jax.ShapeDtypeStruct((b, rows, d), F32)
    return pl.pallas_call(
        _scan_kernel,
        grid=(b, nc),
        in_specs=[sf] * 6 + [sr] * 6,
        out_specs=[sf, sr],
        out_shape=[out, out],
        scratch_shapes=[pltpu.VMEM((2, d // (2 * HEAD), 2 * HEAD, 2 * HEAD), F32)],
        compiler_params=_cparams("parallel", "arbitrary"),
    )(r, v, kk, lw0, k0, e0, r, v, kk, lw1, k1, e1)


def _post_mixer(z, x, m, wo_ref, lnv):
    o = _dot(z.astype(BF16), wo_ref[...])
    x1 = _layer_norm(lnv[2:3] * x + m[2:3] * o, lnv[0:1], lnv[1:2])
    return x1, x1 * (1 + m[4:5]) + m[3:4]


def _rwkv_out_kernel(yf_ref, yr_ref, bonus_ref, g_ref, x_ref, mod_ref, lnx_ref, wo_ref, lnv_ref,
                     x1_ref, h2_ref):
    y = yf_ref[0] + yr_ref[0]
    mu = _head_sum(y) * (1.0 / HEAD)
    yc = y - mu
    var = _head_sum(yc * yc) * (1.0 / HEAD)
    lnx = lnx_ref[...]
    yn = yc * lax.rsqrt(var + RW_LNX_EPS) * lnx[0:1] + lnx[1:2]
    z = (yn + bonus_ref[0]) * g_ref[0]
    x1, h2 = _post_mixer(z, x_ref[0], mod_ref[0], wo_ref, lnv_ref[...])
    x1_ref[0] = x1
    h2_ref[0] = h2


def _rwkv_out(yf, yr, bonus, g, xa, mods, nctx_blk, lnx, wo, lnv):
    b, r, d = xa.shape
    tm = ROW_TILE
    row = pl.BlockSpec((1, tm, d), lambda bb, i: (bb, i, 0))
    full = lambda a: pl.BlockSpec(a.shape, lambda bb, i: (0,) * a.ndim)
    out = jax.ShapeDtypeStruct((b, r, d), F32)
    return pl.pallas_call(
        _rwkv_out_kernel,
        grid=(b, r // tm),
        in_specs=[row] * 5 + [pl.BlockSpec((1, N_MOD, d), lambda bb, i: (bb * 2 + (i >= nctx_blk).astype(I32), 0, 0)),
                              full(lnx), full(wo), full(lnv)],
        out_specs=[row, row],
        out_shape=[out, out],
        compiler_params=_cparams("parallel", "arbitrary"),
    )(yf, yr, bonus, g, xa, mods, lnx, wo, lnv)


def _attn_out_kernel(o_ref, x_ref, mod_ref, wo_ref, lnv_ref, x1_ref, h2_ref):
    x1, h2 = _post_mixer(o_ref[0], x_ref[0], mod_ref[0], wo_ref, lnv_ref[...])
    x1_ref[0] = x1
    h2_ref[0] = h2


def _attn_out(o, xa, mods, blk_off, wo, lnv):
    b, t, d = o.shape
    tm = ROW_TILE
    row = pl.BlockSpec((1, tm, d), lambda bb, i: (bb, i, 0))
    full = lambda a: pl.BlockSpec(a.shape, lambda bb, i: (0,) * a.ndim)
    out = jax.ShapeDtypeStruct((b, t, d), F32)
    return pl.pallas_call(
        _attn_out_kernel,
        grid=(b, t // tm),
        in_specs=[row, pl.BlockSpec((1, tm, d), lambda bb, i: (bb, i + blk_off, 0)),
                  pl.BlockSpec((1, N_MOD, d), lambda bb, i: (bb * 2 + 1, 0, 0)),
                  full(wo), full(lnv)],
        out_specs=[row, row],
        out_shape=[out, out],
        compiler_params=_cparams("parallel", "arbitrary"),
    )(o, xa, mods, wo, lnv)


def _router_kernel(h_ref, rw_ref, aff_ref):
    logits = jnp.dot(h_ref[0], rw_ref[...], precision=HIGHEST, preferred_element_type=F32)
    lane = lax.broadcasted_iota(I32, logits.shape, 1)
    logits = jnp.where(lane < N_EXPERTS, logits, NEG)
    mx = jnp.max(logits, axis=-1, keepdims=True)
    ex = jnp.exp(logits - mx)
    aff_ref[0] = ex / jnp.sum(ex, axis=-1, keepdims=True)


def _router(h2, rw_pad):
    b, r, d = h2.shape
    tm = ROW_TILE
    return pl.pallas_call(
        _router_kernel,
        grid=(b, r // tm),
        in_specs=[pl.BlockSpec((1, tm, d), lambda bb, i: (bb, i, 0)),
                  pl.BlockSpec(rw_pad.shape, lambda bb, i: (0, 0))],
        out_specs=pl.BlockSpec((1, tm, LANES), lambda bb, i: (bb, i, 0)),
        out_shape=jax.ShapeDtypeStruct((b, r, LANES), F32),
        compiler_params=_cparams("parallel", "arbitrary"),
    )(h2, rw_pad)


def _select_kernel(cap, aff_ref, idx_ref, gate_ref, c_ref, sa_ref):
    tn = aff_ref.shape[1]
    blk = LANES
    def bit_step(j, bits):
        cand = bits | jnp.left_shift(jnp.int32(1), 30 - j)
        cnt = jnp.sum((aff_ref[0] >= lax.bitcast_convert_type(cand, F32)).astype(F32), axis=0, keepdims=True)
        return jnp.where(cnt >= cap, cand, bits)

    bits = lax.fori_loop(0, 31, bit_step, jnp.zeros((1, LANES), I32))
    thr = lax.bitcast_convert_type(bits, F32)
    need = cap - jnp.sum((aff_ref[0] > thr).astype(F32), axis=0, keepdims=True)
    tril = (lax.broadcasted_iota(I32, (blk, blk), 1) <= lax.broadcasted_iota(I32, (blk, blk), 0)).astype(BF16)

    def blk_step(j, carry):
        tie_off, sel_off = carry
        rows = pl.ds(pl.multiple_of(j * blk, blk), blk)
        a = aff_ref[0, rows, :]
        eq = (a == thr).astype(F32)
        tie_rank = _dot(tril, eq.astype(BF16)) + tie_off - eq
        sel = ((a > thr) | ((eq > 0) & (tie_rank < need))).astype(F32)
        cinc = _dot(tril, sel.astype(BF16)) + sel_off
        c_ref[rows, :] = sel * cinc
        sa_ref[rows, :] = sel * a
        return (tie_off + jnp.sum(eq, axis=0, keepdims=True), sel_off + jnp.sum(sel, axis=0, keepdims=True))

    zero = jnp.zeros((1, LANES), F32)
    lax.fori_loop(0, tn // blk, blk_step, (zero, zero))

    tc = min(tn, 256)
    slot1 = lax.broadcasted_iota(I32, (1, cap), 1).astype(F32) + 1.0
    lane = lax.broadcasted_iota(I32, (1, LANES), 1)
    for e in range(N_EXPERTS):
        def chunk(j, acc):
            rows = pl.ds(pl.multiple_of(j * tc, tc), tc)
            onehot = (c_ref[rows, e:e + 1] == slot1).astype(BF16)
            ae = sa_ref[rows, e:e + 1]
            tok = lax.broadcasted_iota(I32, (tc, 1), 0) + j * tc
            a_hi = ae.astype(BF16).astype(F32)
            a_mid = (ae - a_hi).astype(BF16).astype(F32)
            pieces = ((tok // 64).astype(F32), (tok % 64).astype(F32), a_hi, a_mid, ae - a_hi - a_mid)
            lhs = jnp.zeros((tc, LANES), F32)
            for n, piece in enumerate(pieces):
                lhs = jnp.where(lane == n, piece, lhs)
            return acc + _dot(lhs.T.astype(BF16), onehot)

        acc = lax.fori_loop(0, tn // tc, chunk, jnp.zeros((LANES, cap), F32))
        idx_ref[0, e:e + 1, :] = (acc[0:1] * 64.0 + acc[1:2]).astype(I32)
        gate_ref[0, e:e + 1, :] = acc[2:3] + acc[3:4] + acc[4:5]


def _select(aff, cap):
    b, tn, _ = aff.shape
    return pl.pallas_call(
        functools.partial(_select_kernel, cap),
        grid=(b,),
        in_specs=[pl.BlockSpec((1, tn, LANES), lambda bb: (bb, 0, 0))],
        out_specs=[pl.BlockSpec((1, N_EXPERTS, cap), lambda bb: (bb, 0, 0))] * 2,
        out_shape=[jax.ShapeDtypeStruct((b, N_EXPERTS, cap), I32),
                   jax.ShapeDtypeStruct((b, N_EXPERTS, cap), F32)],
        scratch_shapes=[pltpu.VMEM((tn, LANES), F32), pltpu.VMEM((tn, LANES), F32)],
        compiler_params=_cparams("parallel"),
    )(aff)


def _expert_kernel(idx_ref, h_hbm, gate_ref, wg_ref, wu_ref, wd_ref, acc_in, acc_hbm,
                   stage, xin, yacc, sem):
    del acc_in
    e = pl.program_id(0)
    f = pl.program_id(1)
    m = stage.shape[0]
    base = e * m

    def rows_loop(issue_one):
        def body(i, carry):
            for u in range(DMA_UNROLL):
                issue_one(i * DMA_UNROLL + u)
            return carry

        lax.fori_loop(0, m // DMA_UNROLL, body, 0)

    def gather(src):
        rows_loop(lambda i: pltpu.make_async_copy(
            src.at[pl.ds(idx_ref[base + i], 1)], stage.at[pl.ds(i, 1)], sem).start())
        pltpu.make_async_copy(src.at[pl.ds(0, m)], stage, sem).wait()

    @pl.when(f == 0)
    def _():
        gather(h_hbm)
        xin[...] = stage[...].astype(BF16)

    x = xin[...]
    a = _dot(x, wg_ref[0, 0].astype(BF16))
    u = _dot(x, wu_ref[0, 0].astype(BF16))
    hm = (a * jax.nn.sigmoid(a) * u).astype(BF16)
    part = _dot(hm, wd_ref[0, 0].astype(BF16))

    @pl.when(f == 0)
    def _():
        yacc[...] = part

    @pl.when(f > 0)
    def _():
        yacc[...] += part

    @pl.when(f == pl.num_programs(1) - 1)
    def _():
        gather(acc_hbm)
        stage[...] = stage[...] + gate_ref[0] * yacc[...]
        rows_loop(lambda i: pltpu.make_async_copy(
            stage.at[pl.ds(i, 1)], acc_hbm.at[pl.ds(idx_ref[base + i], 1)], sem).start())
        pltpu.make_async_copy(stage, acc_hbm.at[pl.ds(0, m)], sem).wait()


def _experts(idx_flat, gates, h_rows, acc, layer, w_gate, w_up, w_down):
    _, n_e, d, f = w_gate.shape
    m = gates.shape[1]
    assert m % DMA_UNROLL == 0
    tf = 512
    grid_spec = pltpu.PrefetchScalarGridSpec(
        num_scalar_prefetch=1,
        grid=(n_e, f // tf),
        in_specs=[pl.BlockSpec(memory_space=pl.ANY),
                  pl.BlockSpec((1, m, 1), lambda e, j, idx: (e, 0, 0)),
                  pl.BlockSpec((1, 1, d, tf), lambda e, j, idx: (layer, e, 0, j)),
                  pl.BlockSpec((1, 1, d, tf), lambda e, j, idx: (layer, e, 0, j)),
                  pl.BlockSpec((1, 1, tf, d), lambda e, j, idx: (layer, e, j, 0)),
                  pl.BlockSpec(memory_space=pl.ANY)],
        out_specs=pl.BlockSpec(memory_space=pl.ANY),
        scratch_shapes=[pltpu.VMEM((m, d), F32), pltpu.VMEM((m, d), BF16), pltpu.VMEM((m, d), F32),
                        pltpu.SemaphoreType.DMA(())],
    )
    return pl.pallas_call(
        _expert_kernel,
        grid_spec=grid_spec,
        out_shape=jax.ShapeDtypeStruct(acc.shape, F32),
        input_output_aliases={6: 0},
        compiler_params=_cparams("arbitrary", "arbitrary"),
    )(idx_flat, h_rows, gates, w_gate, w_up, w_down, acc)


def _moe_ln_kernel(x_ref, moe_ref, mod_ref, lnv_ref, o_ref):
    lnv = lnv_ref[...]
    m = mod_ref[0]
    o_ref[0] = _layer_norm(lnv[2:3] * x_ref[0] + m[5:6] * moe_ref[0], lnv[0:1], lnv[1:2])


def _moe_ln(x1, moe, mods, nctx_blk, lnv):
    b, r, d = x1.shape
    tm = ROW_TILE
    row = pl.BlockSpec((1, tm, d), lambda bb, i: (bb, i, 0))
    return pl.pallas_call(
        _moe_ln_kernel,
        grid=(b, r // tm),
        in_specs=[row, row,
                  pl.BlockSpec((1, N_MOD, d), lambda bb, i: (bb * 2 + (i >= nctx_blk).astype(I32), 0, 0)),
                  pl.BlockSpec(lnv.shape, lambda bb, i: (0, 0))],
        out_specs=row,
        out_shape=jax.ShapeDtypeStruct((b, r, d), F32),
        compiler_params=_cparams("parallel", "arbitrary"),
    )(x1, moe, mods, lnv)


def _ec_moe(h2, streams, rw_pad, layer, w_gate, w_up, w_down):
    b, r, d = h2.shape
    aff = _router(h2, rw_pad)
    idx_parts, gate_parts = [], []
    for off, tn in streams:
        cap = EC_CAPACITY * tn // N_EXPERTS
        idx, gates = _select(lax.slice_in_dim(aff, off, off + tn, axis=1), cap)
        rows = idx + (jnp.arange(b, dtype=I32) * r + off)[:, None, None]
        idx_parts.append(jnp.swapaxes(rows, 0, 1).reshape(N_EXPERTS, b * cap))
        gate_parts.append(jnp.swapaxes(gates, 0, 1).reshape(N_EXPERTS, b * cap))
    idx_flat = jnp.concatenate(idx_parts, axis=1)
    gates = jnp.concatenate(gate_parts, axis=1)
    acc = _experts(idx_flat.reshape(-1), gates[..., None], h2.reshape(b * r, d),
                   jnp.zeros((b * r, d), F32), layer, w_gate, w_up, w_down)
    return acc.reshape(b, r, d)


def _qkv_kernel(x_ref, mod_ref, cos_ref, sin_ref, wq_ref, wk_ref, wv_ref, q_ref, k_ref, v_ref):
    m = mod_ref[0]
    h = (x_ref[0] * (1 + m[1:2]) + m[0:1]).astype(BF16)
    d = x_ref.shape[2]
    rep = d // LANES
    cos = jnp.concatenate([cos_ref[...]] * rep, axis=1)
    sin = jnp.concatenate([sin_ref[...]] * rep, axis=1)
    lane = lax.broadcasted_iota(I32, (1, d), 1)
    lower = (lane % 32) < 16

    def rope(t):
        partner = jnp.where(lower, pltpu.roll(t, d - 16, 1), pltpu.roll(t, 16, 1))
        return t * cos + partner * sin

    q_ref[0] = (rope(_dot(h, wq_ref[...])) * (HEAD ** -0.5)).astype(BF16)
    k_ref[0] = rope(_dot(h, wk_ref[...])).astype(BF16)
    v_ref[0] = _dot(h, wv_ref[...]).astype(BF16)


def _qkv(xa, mods, nctx_blk, cos, sin, wq, wk4, wv4):
    b, r, d = xa.shape
    tm = ROW_TILE
    row = pl.BlockSpec((1, tm, d), lambda bb, i: (bb, i, 0))
    tab = pl.BlockSpec((tm, LANES), lambda bb, i: (i, 0))
    full = lambda a: pl.BlockSpec(a.shape, lambda bb, i: (0,) * a.ndim)
    out = jax.ShapeDtypeStruct((b, r, d), BF16)
    return pl.pallas_call(
        _qkv_kernel,
        grid=(b, r // tm),
        in_specs=[row, pl.BlockSpec((1, N_MOD, d), lambda bb, i: (bb * 2 + (i >= nctx_blk).astype(I32), 0, 0)),
                  tab, tab, full(wq), full(wk4), full(wv4)],
        out_specs=[row] * 3,
        out_shape=[out] * 3,
        compiler_params=_cparams("parallel", "arbitrary"),
    )(xa, mods, cos, sin, wq, wk4, wv4)


def _attn_kernel(n_ctx, sink_ref, q_ref, kc_ref, vc_ref, kp_ref, k0_ref, kn_ref, vp_ref, v0_ref, vn_ref, o_ref):
    nb = pl.program_id(1)
    n_blk = pl.num_programs(1)
    bq = q_ref.shape[1]
    gw = ATT_GROUP * HEAD
    nk = n_ctx + 3 * bq
    col = lax.broadcasted_iota(I32, (bq, nk), 1)
    qpos = nb * bq + lax.broadcasted_iota(I32, (bq, nk), 0)
    kpos = (nb - 1) * bq + (col - n_ctx)
    valid = (col < n_ctx) | ((kpos >= 0) & (kpos < n_blk * bq) & (jnp.abs(qpos - kpos) <= WINDOW))
    lane = lax.broadcasted_iota(I32, (1, gw), 1)
    heads = []
    for h in range(ATT_KV_HEADS):
        sl = slice(gw * h, gw * (h + 1))
        keys = jnp.concatenate([kc_ref[0, :, sl], kp_ref[0, :, sl], k0_ref[0, :, sl], kn_ref[0, :, sl]], axis=0)
        vals = jnp.concatenate([vc_ref[0, :, sl], vp_ref[0, :, sl], v0_ref[0, :, sl], vn_ref[0, :, sl]], axis=0)
        qh = q_ref[0, :, sl]
        scores = [_dot_nt(jnp.where((lane // HEAD) == g, qh, jnp.zeros_like(qh)), keys)
                  for g in range(ATT_GROUP)]
        heads.append((sl, vals, scores))
    probs = []
    for h, (sl, vals, scores) in enumerate(heads):
        row = []
        for g, s in enumerate(scores):
            s = jnp.where(valid, s, NEG)
            sink = sink_ref[h * ATT_GROUP + g]
            mx = jnp.maximum(jnp.max(s, axis=-1, keepdims=True), sink)
            p = jnp.exp(s - mx)
            den = jnp.sum(p, axis=-1, keepdims=True) + jnp.exp(sink - mx)
            row.append((p / den).astype(BF16))
        probs.append(row)
    for (sl, vals, _), row in zip(heads, probs):
        acc = jnp.zeros((bq, gw), F32)
        for g, p in enumerate(row):
            acc = acc + jnp.where((lane // HEAD) == g, _dot(p, vals), 0.0)
        o_ref[0, :, sl] = acc.astype(BF16)


def _attention(q, k4, v4, sink, n_ctx):
    b, r, d = q.shape
    bq = WINDOW
    t = r - n_ctx
    nb = t // bq
    off = n_ctx // bq
    blk = lambda f: pl.BlockSpec((1, bq, d), f)
    ctx_spec = pl.BlockSpec((1, n_ctx, d), lambda bb, i: (bb, 0, 0))
    prev = lambda bb, i: (bb, jnp.maximum(i - 1, 0) + off, 0)
    cur = lambda bb, i: (bb, i + off, 0)
    nxt = lambda bb, i: (bb, jnp.minimum(i + 1, nb - 1) + off, 0)
    return pl.pallas_call(
        functools.partial(_attn_kernel, n_ctx),
        grid=(b, nb),
        in_specs=[pl.BlockSpec(memory_space=pltpu.SMEM), blk(cur), ctx_spec, ctx_spec,
                  blk(prev), blk(cur), blk(nxt), blk(prev), blk(cur), blk(nxt)],
        out_specs=pl.BlockSpec((1, bq, d), lambda bb, i: (bb, i, 0)),
        out_shape=jax.ShapeDtypeStruct((b, t, d), BF16),
        compiler_params=_cparams("parallel", "arbitrary"),
    )(sink, q, k4, v4, k4, k4, k4, v4, v4, v4)


def _rope_tables(n_ctx, t):
    half = HEAD // 2
    inv = 1.0 / (ROPE_BASE ** (jnp.arange(0, half, 2, dtype=F32) / half))
    pos = jnp.arange(t)
    ang_r = (pos // GRID_W).astype(F32)[:, None] * inv
    ang_c = (pos % GRID_W).astype(F32)[:, None] * inv
    cos = jnp.concatenate([jnp.cos(ang_r)] * 2 + [jnp.cos(ang_c)] * 2, axis=1)
    sin = jnp.concatenate([-jnp.sin(ang_r), jnp.sin(ang_r), -jnp.sin(ang_c), jnp.sin(ang_c)], axis=1)
    cos = jnp.concatenate([jnp.ones((n_ctx, HEAD), F32), cos], axis=0)
    sin = jnp.concatenate([jnp.zeros((n_ctx, HEAD), F32), sin], axis=0)
    return jnp.tile(cos, (1, 2)), jnp.tile(sin, (1, 2))


def kernel(x, c, ctx, c_ctx, ada_w, ada_b, ln1_g, ln1_b, ln2_g, ln2_b, rw_mu, rw_wr, rw_wk, rw_wv, rw_w0, rw_w1, rw_w2, rw_a0, rw_a1, rw_a2, rw_g1, rw_g2, rw_kk, rw_ka, rw_rk, rw_lnx_g, rw_lnx_b, rw_wo, at_wqkv, at_wo, at_sink, router_w, moe_w_gate, moe_w_up, moe_w_down):
    b, t, d = x.shape
    n_ctx = ctx.shape[1]
    depth = ada_w.shape[0]
    alpha = (2 * depth) ** 0.25
    assert n_ctx % ROW_TILE == 0 and t % ROW_TILE == 0 and b + 1 <= 8
    nctx_blk = n_ctx // ROW_TILE

    cc = jnp.concatenate([c, c_ctx[None], jnp.zeros((8 - b - 1, d), F32)], axis=0)
    mod_all = _ada_mod(cc, ada_w, ada_b).reshape(depth, 8, N_MOD, d)
    xa = jnp.concatenate([ctx, x], axis=1)
    bf = lambda w: w.astype(BF16)
    pad_rows = lambda w, lo: jnp.pad(w, ((lo, LANES - lo - w.shape[0]), (0, 0)))

    for i in range(depth):
        last = i == depth - 1
        j = i // 2
        mods = jnp.stack([jnp.broadcast_to(mod_all[i, b], (b, N_MOD, d)), mod_all[i, :b]], axis=1).reshape(2 * b, N_MOD, d)
        lnv1 = jnp.stack([ln1_g[i], ln1_b[i], jnp.full((d,), alpha, F32)])
        lnv2 = jnp.stack([ln2_g[i], ln2_b[i], jnp.full((d,), alpha, F32)])
        rw_pad = jnp.pad(router_w[i], ((0, 0), (0, LANES - N_EXPERTS)))
        if i % 2 == 0:
            vec = jnp.concatenate([rw_mu[j], rw_w0[j], rw_a0[j], rw_kk[j][None], rw_ka[j][None],
                                   rw_rk[j].reshape(1, d), jnp.zeros((3, d), F32)], axis=0)
            w1 = bf(jnp.concatenate([rw_w1[j, 0], rw_w1[j, 1]], axis=1))
            a1 = bf(jnp.concatenate([rw_a1[j, 0], rw_a1[j, 1]], axis=1))
            w2 = bf(jnp.stack([pad_rows(rw_w2[j, 0], 0), pad_rows(rw_w2[j, 1], HEAD)]))
            a2 = bf(jnp.stack([pad_rows(rw_a2[j, 0], 0), pad_rows(rw_a2[j, 1], HEAD)]))
            (r, v, kk, g, bonus, lw0, k0, e0, lw1, k1, e1) = _rwkv_proj(
                xa, mods, nctx_blk, vec, bf(rw_wr[j]), bf(rw_wk[j]), bf(rw_wv[j]), w1, a1, bf(rw_g1[j]),
                w2, a2, bf(rw_g2[j]))
            yf, yr = _rwkv_scan(r, v, kk, lw0, k0, e0, lw1, k1, e1, n_ctx // CHUNK)
            lnx = jnp.stack([rw_lnx_g[j], rw_lnx_b[j]])
            x1, h2 = _rwkv_out(yf, yr, bonus, g, xa, mods, nctx_blk, lnx, bf(rw_wo[j]), lnv1)
        else:
            nq = ATT_KV_HEADS * ATT_GROUP * HEAD
            nk = ATT_KV_HEADS * HEAD
            wqkv = at_wqkv[j]
            tile4 = lambda w: jnp.tile(w.reshape(d, ATT_KV_HEADS, 1, HEAD), (1, 1, ATT_GROUP, 1)).reshape(d, nq)
            cos, sin = _rope_tables(n_ctx, t)
            q, k4, v4 = _qkv(xa, mods, nctx_blk, cos, sin, bf(wqkv[:, :nq]),
                             bf(tile4(wqkv[:, nq:nq + nk])), bf(tile4(wqkv[:, nq + nk:])))
            o = _attention(q, k4, v4, at_sink[j], n_ctx)
            x1, h2 = _attn_out(o, xa, mods, nctx_blk, bf(at_wo[j]), lnv1)
        if last:
            if x1.shape[1] != t:
                x1, h2 = x1[:, n_ctx:], h2[:, n_ctx:]
            moe = _ec_moe(h2, [(0, t)], rw_pad, i, moe_w_gate, moe_w_up, moe_w_down)
            return _moe_ln(x1, moe, mods, 0, lnv2)
        moe = _ec_moe(h2, [(n_ctx, t), (0, n_ctx)], rw_pad, i, moe_w_gate, moe_w_up, moe_w_down)
        xa = _moe_ln(x1, moe, mods, nctx_blk, lnv2)
    return xa[:, n_ctx:]
```

```python
import functools

import jax
import jax.numpy as jnp
from jax import lax
from jax.experimental import pallas as pl
from jax.experimental.pallas import tpu as pltpu

F32 = jnp.float32
BF16 = jnp.bfloat16
I32 = jnp.int32
HIGHEST = lax.Precision.HIGHEST

LANES = 128
LN_EPS = 1e-5
N_MOD = 6
HEAD = 64
RW_LNX_EPS = 64e-5
RW_DECAY_SCALE = 0.606531
ATT_KV_HEADS = 4
ATT_GROUP = 4
WINDOW = 128
GRID_W = 64
ROPE_BASE = 10000.0
NEG = -1e30
N_EXPERTS = 16
EC_CAPACITY = 2
CHUNK = 64
DMA_UNROLL = 8
ROW_TILE = 256
VMEM_LIMIT = 56 * 2 ** 20


def _cparams(*sem):
    return pltpu.CompilerParams(dimension_semantics=sem, vmem_limit_bytes=VMEM_LIMIT)


def _dot(a, b):
    return jnp.dot(a, b, preferred_element_type=F32)


def _dot_nt(a, b):
    return lax.dot_general(a, b, (((1,), (1,)), ((), ())), preferred_element_type=F32)


def _split(x):
    hi = x.astype(BF16)
    lo = (x - hi.astype(F32)).astype(BF16)
    return hi, lo


def _head_sum(x):
    d = x.shape[-1]
    gr = (lax.broadcasted_iota(I32, (d, LANES), 0) // HEAD
          == lax.broadcasted_iota(I32, (d, LANES), 1)).astype(BF16)
    ge = (lax.broadcasted_iota(I32, (LANES, d), 0)
          == lax.broadcasted_iota(I32, (LANES, d), 1) // HEAD).astype(BF16)
    hi, lo = _split(x)
    s = _dot(hi, gr) + _dot(lo, gr)
    shi, slo = _split(s)
    return _dot(shi, ge) + _dot(slo, ge)


def _layer_norm(x, g, b):
    mu = jnp.mean(x, axis=-1, keepdims=True)
    xc = x - mu
    var = jnp.mean(xc * xc, axis=-1, keepdims=True)
    return xc * lax.rsqrt(var + LN_EPS) * g + b


def _ada_kernel(cc_ref, w_ref, b_ref, o_ref):
    cc = cc_ref[...]
    s = cc * jax.nn.sigmoid(cc)
    o_ref[0] = jnp.dot(s, w_ref[0], precision=HIGHEST, preferred_element_type=F32) + b_ref[0]


def _ada_mod(cc, ada_w, ada_b):
    depth, d, n = ada_w.shape
    tn = n // 4
    return pl.pallas_call(
        _ada_kernel,
        grid=(depth, n // tn),
        in_specs=[pl.BlockSpec((8, d), lambda l, j: (0, 0)),
                  pl.BlockSpec((1, d, tn), lambda l, j: (l, 0, j)),
                  pl.BlockSpec((1, 1, tn), lambda l, j: (l, 0, j))],
        out_specs=pl.BlockSpec((1, 8, tn), lambda l, j: (l, 0, j)),
        out_shape=jax.ShapeDtypeStruct((depth, 8, n), F32),
        compiler_params=_cparams("arbitrary", "arbitrary"),
    )(cc, ada_w, ada_b.reshape(depth, 1, n))


def _rwkv_proj_kernel(nctx_blk, x_ref, xp_ref, xn_ref, mod_ref, vec_ref, wr_ref, wk_ref, wv_ref,
                      w1_ref, a1_ref, g1_ref, w2_ref, a2_ref, g2_ref,
                      r_ref, v_ref, kk_ref, g_ref, bonus_ref,
                      lw0_ref, k0_ref, e0_ref, lw1_ref, k1_ref, e1_ref):
    i = pl.program_id(1)
    nblk = pl.num_programs(1)
    tm = x_ref.shape[1]
    m = mod_ref[0]
    shift, scale = m[0:1], m[1:2]
    h = x_ref[0] * (1 + scale) + shift
    first = (i == 0) | (i == nctx_blk)
    last = (i == nctx_blk - 1) | (i == nblk - 1)
    hp = jnp.where(first, 0.0, xp_ref[0][7:8] * (1 + scale) + shift)
    hn = jnp.where(last, 0.0, xn_ref[0][0:1] * (1 + scale) + shift)
    rows = lax.broadcasted_iota(I32, (tm, 1), 0)
    prev = jnp.where(rows == 0, hp, pltpu.roll(h, 1, 0))
    nxt = jnp.where(rows == tm - 1, hn, pltpu.roll(h, tm - 1, 0))
    xx = 0.5 * (prev + nxt) - h
    vec = vec_ref[...]
    mix = lambda j: (h + xx * vec[j:j + 1]).astype(BF16)
    r = _dot(mix(0), wr_ref[...])
    k = _dot(mix(2), wk_ref[...])
    v = _dot(mix(3), wv_ref[...])
    tw = jnp.tanh(_dot(mix(1), w1_ref[...])).astype(BF16)
    ta = _dot(mix(4), a1_ref[...]).astype(BF16)
    g = _dot(jax.nn.sigmoid(_dot(mix(5), g1_ref[...])).astype(BF16), g2_ref[...])
    kk = k * vec[10:11]
    nrm = jnp.sqrt(_head_sum(kk * kk))
    kk = kk / jnp.maximum(nrm, 1e-12)
    bonus = _head_sum(r * k * vec[12:13]) * v
    r_ref[0] = r
    v_ref[0] = v
    kk_ref[0] = kk
    g_ref[0] = g
    bonus_ref[0] = bonus
    for d, (lw_ref, kd_ref, e_ref) in enumerate(((lw0_ref, k0_ref, e0_ref), (lw1_ref, k1_ref, e1_ref))):
        lw_ref[0] = -RW_DECAY_SCALE * jax.nn.sigmoid(vec[6 + d:7 + d] + _dot(tw, w2_ref[d]))
        eta = jax.nn.sigmoid(vec[8 + d:9 + d] + _dot(ta, a2_ref[d]))
        e_ref[0] = eta
        kd_ref[0] = k * (1 + (eta - 1) * vec[11:12])


def _rwkv_proj(xa, mods, nctx_blk, vec, wr, wk, wv, w1, a1, g1, w2, a2, g2):
    b, r, d = xa.shape
    tm = ROW_TILE
    nblk = r // tm
    hb = tm // 8
    row = pl.BlockSpec((1, tm, d), lambda bb, i: (bb, i, 0))
    full = lambda a: pl.BlockSpec(a.shape, lambda bb, i: (0,) * a.ndim)
    out = jax.ShapeDtypeStruct((b, r, d), F32)
    return pl.pallas_call(
        functools.partial(_rwkv_proj_kernel, nctx_blk),
        grid=(b, nblk),
        in_specs=[row,
                  pl.BlockSpec((1, 8, d), lambda bb, i: (bb, jnp.maximum(i * hb - 1, 0), 0)),
                  pl.BlockSpec((1, 8, d), lambda bb, i: (bb, jnp.minimum((i + 1) * hb, r // 8 - 1), 0)),
                  pl.BlockSpec((1, N_MOD, d), lambda bb, i: (bb * 2 + (i >= nctx_blk).astype(I32), 0, 0)),
                  full(vec), full(wr), full(wk), full(wv), full(w1), full(a1), full(g1),
                  full(w2), full(a2), full(g2)],
        out_specs=[row] * 11,
        out_shape=[out] * 11,
        compiler_params=_cparams("parallel", "arbitrary"),
    )(xa, xa, xa, mods, vec, wr, wk, wv, w1, a1, g1, w2, a2, g2)


def _scan_kernel(rf_ref, vf_ref, kkf_ref, lw0_ref, k0_ref, e0_ref,
                 rr_ref, vr_ref, kkr_ref, lw1_ref, k1_ref, e1_ref,
                 yf_ref, yr_ref, s_ref):
    c = CHUNK
    assert c == HEAD

    @pl.when(pl.program_id(1) == 0)
    def _():
        s_ref[...] = jnp.zeros_like(s_ref)

    ti = lax.broadcasted_iota(I32, (c, c), 0)
    tj = lax.broadcasted_iota(I32, (c, c), 1)
    tt = lax.broadcasted_iota(I32, (c, 2 * c), 0)
    ss = lax.broadcasted_iota(I32, (c, 2 * c), 1) % c
    m0 = lax.broadcasted_iota(I32, (1, 2 * HEAD), 1) < HEAD

    def stack(x):
        return jnp.concatenate([jnp.where(m0, x, 0.0), jnp.where(m0, 0.0, x)], axis=0)

    dirs = ((rf_ref, vf_ref, kkf_ref, lw0_ref, k0_ref, e0_ref, yf_ref),
            (rr_ref, vr_ref, kkr_ref, lw1_ref, k1_ref, e1_ref, yr_ref))
    n_pair = s_ref.shape[1]
    ch = []
    for d, (r_ref, v_ref, kk_ref, lw_ref, kd_ref, e_ref, y_ref) in enumerate(dirs):
        if d == 0:
            tri = (tj <= ti).astype(BF16)
            strict, incl = ss < tt, ss <= tt
        else:
            tri = (tj >= ti).astype(BF16)
            strict, incl = ss > tt, ss >= tt
        lw = lw_ref[0]
        lw_hi = lw.astype(BF16)
        lw_mid = (lw - lw_hi.astype(F32)).astype(BF16)
        lw_lo = (lw - lw_hi.astype(F32) - lw_mid.astype(F32)).astype(BF16)
        cs = _dot(jnp.concatenate([tri, tri, tri], axis=1), jnp.concatenate([lw_hi, lw_mid, lw_lo], axis=0))
        gam = jnp.exp(cs)
        gam_inv = jnp.exp(-cs)
        kk = kk_ref[0]
        a_t = -(kk * jnp.exp(cs - lw))
        b_t = kk * e_ref[0] * gam_inv
        k_t = kd_ref[0] * gam_inv
        r_t = r_ref[0] * gam
        v_all = v_ref[0]
        g_end = gam[c - 1:c] if d == 0 else gam[0:1]
        for p in range(n_pair):
            sl = slice(2 * HEAD * p, 2 * HEAD * (p + 1))
            ch.append(dict(
                d=d, p=p, sl=sl, y_ref=y_ref, strict=strict, incl=incl, v2=stack(v_all[:, sl]),
                ar=jnp.concatenate([a_t[:, sl], r_t[:, sl]], axis=0).astype(BF16),
                bk=jnp.concatenate([stack(b_t[:, sl]), stack(k_t[:, sl])], axis=0),
                ge=g_end[:, sl], s0=s_ref[d, p]))
    for q in ch:
        q["m1"] = _dot_nt(q["ar"], q["bk"].astype(BF16))
    for q in ch:
        q["ars"] = _dot_nt(q["ar"], q["s0"].astype(BF16))
    for q in ch:
        m1 = q.pop("m1")
        q["pw"] = jnp.where(q["strict"], m1[:c, :2 * c], 0.0)
        lak = jnp.where(q["strict"], m1[:c, 2 * c:], 0.0).astype(BF16)
        q["lr"] = jnp.concatenate([jnp.where(q["incl"], m1[c:, :2 * c], 0.0),
                                   jnp.where(q["incl"], m1[c:, 2 * c:], 0.0)], axis=1).astype(BF16)
        q["x"] = q["ars"][:c] + _dot(lak, q["v2"].astype(BF16))
    for it in range(6):
        for q in ch:
            pb = q["pw"].astype(BF16)
            if it < 5:
                px = _dot(pb, jnp.concatenate([stack(q["x"]), stack(q["pw"])], axis=1).astype(BF16))
                q["x"] = q["x"] + px[:, :2 * c]
                q["pw"] = px[:, 2 * c:]
            else:
                q["x"] = q["x"] + _dot(pb, stack(q["x"]).astype(BF16))
    for q in ch:
        q["uv"] = jnp.concatenate([stack(q["x"]), q["v2"]], axis=0)
        q["y_ref"][0, :, q["sl"]] = q["ars"][c:] + _dot(q["lr"], q["uv"].astype(BF16))
    for q in ch:
        s_ref[q["d"], q["p"]] = (q["s0"] * q["ge"]
                                 + _dot(q["uv"].T.astype(BF16), (q["bk"] * q["ge"]).astype(BF16)))


def _rwkv_scan(r, v, kk, lw0, k0, e0, lw1, k1, e1, nctx_chunks):
    b, rows, d = r.shape
    c = CHUNK
    nc = rows // c
    fwd = lambda bb, i: (bb, i, 0)
    rev = lambda bb, i: (bb, jnp.where(i < nctx_chunks, nctx_chunks - 1 - i, nc - 1 - (i - nctx_chunks)), 0)
    sf = pl.BlockSpec((1, c, d), fwd)
    sr = pl.BlockSpec((1, c, d), rev)
    out = jax.ShapeDtypeStruct((b, rows, d), F32)
    return pl.pallas_call(
        _scan_kernel,
        grid=(b, nc),
        in_specs=[sf] * 6 + [sr] * 6,
        out_specs=[sf, sr],
        out_shape=[out, out],
        scratch_shapes=[pltpu.VMEM((2, d // (2 * HEAD), 2 * HEAD, 2 * HEAD), F32)],
        compiler_params=_cparams("parallel", "arbitrary"),
    )(r, v, kk, lw0, k0, e0, r, v, kk, lw1, k1, e1)


def _post_mixer(z, x, m, wo_ref, lnv, rw_ref, x1_ref, h2_ref, afft_ref):
    d = x.shape[-1]
    o = _dot(z.astype(BF16), wo_ref[...])
    x1 = _layer_norm(lnv[2:3] * x + m[2:3] * o, lnv[0:1], lnv[1:2])
    h2 = x1 * (1 + m[4:5]) + m[3:4]
    logits = jnp.dot(h2, rw_ref[...], precision=HIGHEST, preferred_element_type=F32)
    lane = lax.broadcasted_iota(I32, logits.shape, 1)
    logits = jnp.where(lane < N_EXPERTS, logits, NEG)
    ex = jnp.exp(logits - jnp.max(logits, axis=-1, keepdims=True))
    aff = ex / jnp.sum(ex, axis=-1, keepdims=True)
    x1_ref[0] = x1
    h2_ref[0, :, :d] = h2
    h2_ref[0, :, d:] = aff
    afft_ref[0] = aff.T[:N_EXPERTS]


def _rwkv_out_kernel(yf_ref, yr_ref, bonus_ref, g_ref, x_ref, mod_ref, lnx_ref, wo_ref, lnv_ref, rw_ref,
                     x1_ref, h2_ref, afft_ref):
    y = yf_ref[0] + yr_ref[0]
    mu = _head_sum(y) * (1.0 / HEAD)
    yc = y - mu
    var = _head_sum(yc * yc) * (1.0 / HEAD)
    lnx = lnx_ref[...]
    yn = yc * lax.rsqrt(var + RW_LNX_EPS) * lnx[0:1] + lnx[1:2]
    z = (yn + bonus_ref[0]) * g_ref[0]
    _post_mixer(z, x_ref[0], mod_ref[0], wo_ref, lnv_ref[...], rw_ref, x1_ref, h2_ref, afft_ref)


def _post_mixer_outs(b, rows, d, tm):
    specs = [pl.BlockSpec((1, tm, d), lambda bb, i: (bb, i, 0)),
             pl.BlockSpec((1, tm, d + LANES), lambda bb, i: (bb, i, 0)),
             pl.BlockSpec((1, N_EXPERTS, tm), lambda bb, i: (bb, 0, i))]
    shapes = [jax.ShapeDtypeStruct((b, rows, d), F32), jax.ShapeDtypeStruct((b, rows, d + LANES), F32),
              jax.ShapeDtypeStruct((b, N_EXPERTS, rows), F32)]
    return specs, shapes


def _rwkv_out(yf, yr, bonus, g, xa, mods, nctx_blk, lnx, wo, lnv, rw_pad):
    b, r, d = xa.shape
    tm = ROW_TILE
    row = pl.BlockSpec((1, tm, d), lambda bb, i: (bb, i, 0))
    full = lambda a: pl.BlockSpec(a.shape, lambda bb, i: (0,) * a.ndim)
    out_specs, out_shape = _post_mixer_outs(b, r, d, tm)
    return pl.pallas_call(
        _rwkv_out_kernel,
        grid=(b, r // tm),
        in_specs=[row] * 5 + [pl.BlockSpec((1, N_MOD, d), lambda bb, i: (bb * 2 + (i >= nctx_blk).astype(I32), 0, 0)),
                              full(lnx), full(wo), full(lnv), full(rw_pad)],
        out_specs=out_specs,
        out_shape=out_shape,
        compiler_params=_cparams("parallel", "arbitrary"),
    )(yf, yr, bonus, g, xa, mods, lnx, wo, lnv, rw_pad)


def _attn_out_kernel(o_ref, x_ref, mod_ref, wo_ref, lnv_ref, rw_ref, x1_ref, h2_ref, afft_ref):
    _post_mixer(o_ref[0], x_ref[0], mod_ref[0], wo_ref, lnv_ref[...], rw_ref, x1_ref, h2_ref, afft_ref)


def _attn_out(o, xa, mods, blk_off, wo, lnv, rw_pad):
    b, t, d = o.shape
    tm = ROW_TILE
    row = pl.BlockSpec((1, tm, d), lambda bb, i: (bb, i, 0))
    full = lambda a: pl.BlockSpec(a.shape, lambda bb, i: (0,) * a.ndim)
    out_specs, out_shape = _post_mixer_outs(b, t, d, tm)
    return pl.pallas_call(
        _attn_out_kernel,
        grid=(b, t // tm),
        in_specs=[row, pl.BlockSpec((1, tm, d), lambda bb, i: (bb, i + blk_off, 0)),
                  pl.BlockSpec((1, N_MOD, d), lambda bb, i: (bb * 2 + 1, 0, 0)),
                  full(wo), full(lnv), full(rw_pad)],
        out_specs=out_specs,
        out_shape=out_shape,
        compiler_params=_cparams("parallel", "arbitrary"),
    )(o, xa, mods, wo, lnv, rw_pad)


def _select_kernel(cap, aff_ref, idx_ref, c_ref):
    n_e, tn = aff_ref.shape[1], aff_ref.shape[2]

    def bit_step(j, bits):
        cand = bits | jnp.left_shift(jnp.int32(1), 30 - j)
        cnt = jnp.sum((aff_ref[0] >= lax.bitcast_convert_type(cand, F32)).astype(F32), axis=1, keepdims=True)
        return jnp.where(cnt >= cap, cand, bits)

    bits = lax.fori_loop(0, 31, bit_step, jnp.zeros((n_e, 1), I32))
    thr = lax.bitcast_convert_type(bits, F32)
    need = cap - jnp.sum((aff_ref[0] > thr).astype(F32), axis=1, keepdims=True)
    triu = (lax.broadcasted_iota(I32, (LANES, LANES), 0) <= lax.broadcasted_iota(I32, (LANES, LANES), 1)).astype(BF16)
    tie_off = jnp.zeros((n_e, 1), F32)
    sel_off = jnp.zeros((n_e, 1), F32)
    for j in range(tn // LANES):
        cols = slice(j * LANES, (j + 1) * LANES)
        a = aff_ref[0, :, cols]
        eq = (a == thr).astype(F32)
        tie_rank = _dot(eq.astype(BF16), triu) + tie_off - eq
        sel = ((a > thr) | ((eq > 0) & (tie_rank < need))).astype(F32)
        c_ref[:, cols] = _dot(sel.astype(BF16), triu) + sel_off
        tie_off = tie_off + jnp.sum(eq, axis=1, keepdims=True)
        sel_off = sel_off + jnp.sum(sel, axis=1, keepdims=True)

    ns = min(cap, 256)
    tc = min(tn, 1024)
    lane = lax.broadcasted_iota(I32, (1, LANES), 1)
    for ci in range(cap // ns):
        slot = (lax.broadcasted_iota(I32, (ns, 1), 0) + ci * ns).astype(F32)

        def per_expert(e, out):
            def chunk(j, part):
                ce = c_ref[pl.ds(e, 1), pl.ds(pl.multiple_of(j * tc, tc), tc)]
                for k in range(tc // LANES):
                    part = part + (ce[:, k * LANES:(k + 1) * LANES] <= slot).astype(F32)
                return part

            part = lax.fori_loop(0, tn // tc, chunk, jnp.zeros((ns, LANES), F32))
            return jnp.where(lane == e, jnp.sum(part, axis=1, keepdims=True), out)

        out = lax.fori_loop(0, n_e, per_expert, jnp.zeros((ns, LANES), F32))
        idx_ref[0, ci * ns:(ci + 1) * ns, :] = out.astype(I32)


def _select(afft, cap):
    b, n_e, tn = afft.shape
    return pl.pallas_call(
        functools.partial(_select_kernel, cap),
        grid=(b,),
        in_specs=[pl.BlockSpec((1, n_e, tn), lambda bb: (bb, 0, 0))],
        out_specs=pl.BlockSpec((1, cap, LANES), lambda bb: (bb, 0, 0)),
        out_shape=jax.ShapeDtypeStruct((b, cap, LANES), I32),
        scratch_shapes=[pltpu.VMEM((n_e, tn), F32)],
        compiler_params=_cparams("parallel"),
    )(afft)


def _expert_kernel(idx_ref, h_hbm, wg_ref, wu_ref, wd_ref, acc_in, acc_hbm,
                   stage, xin, gate, yacc, sem):
    del acc_in
    e = pl.program_id(0)
    f = pl.program_id(1)
    m, d = xin.shape
    base = e * m

    def rows_loop(issue_one):
        def body(i, carry):
            for u in range(DMA_UNROLL):
                issue_one(i * DMA_UNROLL + u)
            return carry

        lax.fori_loop(0, m // DMA_UNROLL, body, 0)

    def gather(src, width):
        rows_loop(lambda i: pltpu.make_async_copy(
            src.at[pl.ds(idx_ref[base + i], 1)], stage.at[pl.ds(i, 1), pl.ds(0, width)], sem).start())
        pltpu.make_async_copy(src.at[pl.ds(0, m)], stage.at[pl.ds(0, m), pl.ds(0, width)], sem).wait()

    @pl.when(f == 0)
    def _():
        gather(h_hbm, stage.shape[1])
        xin[...] = stage[:, :d].astype(BF16)
        lane = lax.broadcasted_iota(I32, (1, LANES), 1)
        gate[...] = jnp.sum(jnp.where(lane == e, stage[:, d:], 0.0), axis=1, keepdims=True)

    x = xin[...]
    a = _dot(x, wg_ref[0, 0].astype(BF16))
    u = _dot(x, wu_ref[0, 0].astype(BF16))
    hm = (a * jax.nn.sigmoid(a) * u).astype(BF16)
    part = _dot(hm, wd_ref[0, 0].astype(BF16))

    @pl.when(f == 0)
    def _():
        yacc[...] = part

    @pl.when(f > 0)
    def _():
        yacc[...] += part

    @pl.when(f == pl.num_programs(1) - 1)
    def _():
        gather(acc_hbm, d)
        stage[:, :d] = stage[:, :d] + gate[...] * yacc[...]
        rows_loop(lambda i: pltpu.make_async_copy(
            stage.at[pl.ds(i, 1), pl.ds(0, d)], acc_hbm.at[pl.ds(idx_ref[base + i], 1)], sem).start())
        pltpu.make_async_copy(stage.at[pl.ds(0, m), pl.ds(0, d)], acc_hbm.at[pl.ds(0, m)], sem).wait()


def _experts(idx_flat, m, h_rows, acc, layer, w_gate, w_up, w_down):
    _, n_e, d, f = w_gate.shape
    assert m % DMA_UNROLL == 0 and h_rows.shape[1] == d + LANES
    tf = 512
    grid_spec = pltpu.PrefetchScalarGridSpec(
        num_scalar_prefetch=1,
        grid=(n_e, f // tf),
        in_specs=[pl.BlockSpec(memory_space=pl.ANY),
                  pl.BlockSpec((1, 1, d, tf), lambda e, j, idx: (layer, e, 0, j)),
                  pl.BlockSpec((1, 1, d, tf), lambda e, j, idx: (layer, e, 0, j)),
                  pl.BlockSpec((1, 1, tf, d), lambda e, j, idx: (layer, e, j, 0)),
                  pl.BlockSpec(memory_space=pl.ANY)],
        out_specs=pl.BlockSpec(memory_space=pl.ANY),
        scratch_shapes=[pltpu.VMEM((m, d + LANES), F32), pltpu.VMEM((m, d), BF16), pltpu.VMEM((m, 1), F32),
                        pltpu.VMEM((m, d), F32), pltpu.SemaphoreType.DMA(())],
    )
    return pl.pallas_call(
        _expert_kernel,
        grid_spec=grid_spec,
        out_shape=jax.ShapeDtypeStruct(acc.shape, F32),
        input_output_aliases={5: 0},
        compiler_params=_cparams("arbitrary", "arbitrary"),
    )(idx_flat, h_rows, w_gate, w_up, w_down, acc)


def _moe_ln_kernel(x_ref, moe_ref, mod_ref, lnv_ref, o_ref):
    lnv = lnv_ref[...]
    m = mod_ref[0]
    o_ref[0] = _layer_norm(lnv[2:3] * x_ref[0] + m[5:6] * moe_ref[0], lnv[0:1], lnv[1:2])


def _moe_ln(x1, moe, mods, nctx_blk, lnv):
    b, r, d = x1.shape
    tm = ROW_TILE
    row = pl.BlockSpec((1, tm, d), lambda bb, i: (bb, i, 0))
    return pl.pallas_call(
        _moe_ln_kernel,
        grid=(b, r // tm),
        in_specs=[row, row,
                  pl.BlockSpec((1, N_MOD, d), lambda bb, i: (bb * 2 + (i >= nctx_blk).astype(I32), 0, 0)),
                  pl.BlockSpec(lnv.shape, lambda bb, i: (0, 0))],
        out_specs=row,
        out_shape=jax.ShapeDtypeStruct((b, r, d), F32),
        compiler_params=_cparams("parallel", "arbitrary"),
    )(x1, moe, mods, lnv)


def _ec_moe(h2x, afft, streams, layer, w_gate, w_up, w_down):
    b, r, dx = h2x.shape
    d = dx - LANES
    idx_parts = []
    for off, tn in streams:
        cap = EC_CAPACITY * tn // N_EXPERTS
        idx = _select(lax.slice_in_dim(afft, off, off + tn, axis=2), cap)[:, :, :N_EXPERTS]
        rows = idx + (jnp.arange(b, dtype=I32) * r + off)[:, None, None]
        idx_parts.append(jnp.transpose(rows, (2, 0, 1)).reshape(N_EXPERTS, b * cap))
    idx_flat = jnp.concatenate(idx_parts, axis=1)
    acc = _experts(idx_flat.reshape(-1), idx_flat.shape[1], h2x.reshape(b * r, dx),
                   jnp.zeros((b * r, d), F32), layer, w_gate, w_up, w_down)
    return acc.reshape(b, r, d)


def _qkv_kernel(x_ref, mod_ref, cos_ref, sin_ref, wq_ref, wk_ref, wv_ref, q_ref, k_ref, v_ref):
    m = mod_ref[0]
    h = (x_ref[0] * (1 + m[1:2]) + m[0:1]).astype(BF16)
    d = x_ref.shape[2]
    rep = d // LANES
    cos = jnp.concatenate([cos_ref[...]] * rep, axis=1)
    sin = jnp.concatenate([sin_ref[...]] * rep, axis=1)
    lane = lax.broadcasted_iota(I32, (1, d), 1)
    lower = (lane % 32) < 16

    def rope(t):
        partner = jnp.where(lower, pltpu.roll(t, d - 16, 1), pltpu.roll(t, 16, 1))
        return t * cos + partner * sin

    q_ref[0] = (rope(_dot(h, wq_ref[...])) * (HEAD ** -0.5)).astype(BF16)
    k_ref[0] = rope(_dot(h, wk_ref[...])).astype(BF16)
    v_ref[0] = _dot(h, wv_ref[...]).astype(BF16)


def _qkv(xa, mods, nctx_blk, cos, sin, wq, wk4, wv4):
    b, r, d = xa.shape
    tm = ROW_TILE
    row = pl.BlockSpec((1, tm, d), lambda bb, i: (bb, i, 0))
    tab = pl.BlockSpec((tm, LANES), lambda bb, i: (i, 0))
    full = lambda a: pl.BlockSpec(a.shape, lambda bb, i: (0,) * a.ndim)
    out = jax.ShapeDtypeStruct((b, r, d), BF16)
    return pl.pallas_call(
        _qkv_kernel,
        grid=(b, r // tm),
        in_specs=[row, pl.BlockSpec((1, N_MOD, d), lambda bb, i: (bb * 2 + (i >= nctx_blk).astype(I32), 0, 0)),
                  tab, tab, full(wq), full(wk4), full(wv4)],
        out_specs=[row] * 3,
        out_shape=[out] * 3,
        compiler_params=_cparams("parallel", "arbitrary"),
    )(xa, mods, cos, sin, wq, wk4, wv4)


def _attn_kernel(n_ctx, sink_ref, q_ref, kc_ref, vc_ref, kp_ref, k0_ref, kn_ref, vp_ref, v0_ref, vn_ref, o_ref):
    nb = pl.program_id(1)
    n_blk = pl.num_programs(1)
    bq = q_ref.shape[1]
    gw = ATT_GROUP * HEAD
    nk = n_ctx + 3 * bq
    col = lax.broadcasted_iota(I32, (bq, nk), 1)
    qpos = nb * bq + lax.broadcasted_iota(I32, (bq, nk), 0)
    kpos = (nb - 1) * bq + (col - n_ctx)
    valid = (col < n_ctx) | ((kpos >= 0) & (kpos < n_blk * bq) & (jnp.abs(qpos - kpos) <= WINDOW))
    lane = lax.broadcasted_iota(I32, (1, gw), 1)
    heads = []
    for h in range(ATT_KV_HEADS):
        sl = slice(gw * h, gw * (h + 1))
        keys = jnp.concatenate([kc_ref[0, :, sl], kp_ref[0, :, sl], k0_ref[0, :, sl], kn_ref[0, :, sl]], axis=0)
        vals = jnp.concatenate([vc_ref[0, :, sl], vp_ref[0, :, sl], v0_ref[0, :, sl], vn_ref[0, :, sl]], axis=0)
        qh = q_ref[0, :, sl]
        scores = [_dot_nt(jnp.where((lane // HEAD) == g, qh, jnp.zeros_like(qh)), keys)
                  for g in range(ATT_GROUP)]
        heads.append((sl, vals, scores))
    probs = []
    for h, (sl, vals, scores) in enumerate(heads):
        row = []
        for g, s in enumerate(scores):
            s = jnp.where(valid, s, NEG)
            sink = sink_ref[h * ATT_GROUP + g]
            mx = jnp.maximum(jnp.max(s, axis=-1, keepdims=True), sink)
            p = jnp.exp(s - mx)
            den = jnp.sum(p, axis=-1, keepdims=True) + jnp.exp(sink - mx)
            row.append((p / den).astype(BF16))
        probs.append(row)
    for (sl, vals, _), row in zip(heads, probs):
        acc = jnp.zeros((bq, gw), F32)
        for g, p in enumerate(row):
            acc = acc + jnp.where((lane // HEAD) == g, _dot(p, vals), 0.0)
        o_ref[0, :, sl] = acc.astype(BF16)


def _attention(q, k4, v4, sink, n_ctx):
    b, r, d = q.shape
    bq = WINDOW
    t = r - n_ctx
    nb = t // bq
    off = n_ctx // bq
    blk = lambda f: pl.BlockSpec((1, bq, d), f)
    ctx_spec = pl.BlockSpec((1, n_ctx, d), lambda bb, i: (bb, 0, 0))
    prev = lambda bb, i: (bb, jnp.maximum(i - 1, 0) + off, 0)
    cur = lambda bb, i: (bb, i + off, 0)
    nxt = lambda bb, i: (bb, jnp.minimum(i + 1, nb - 1) + off, 0)
    return pl.pallas_call(
        functools.partial(_attn_kernel, n_ctx),
        grid=(b, nb),
        in_specs=[pl.BlockSpec(memory_space=pltpu.SMEM), blk(cur), ctx_spec, ctx_spec,
                  blk(prev), blk(cur), blk(nxt), blk(prev), blk(cur), blk(nxt)],
        out_specs=pl.BlockSpec((1, bq, d), lambda bb, i: (bb, i, 0)),
        out_shape=jax.ShapeDtypeStruct((b, t, d), BF16),
        compiler_params=_cparams("parallel", "arbitrary"),
    )(sink, q, k4, v4, k4, k4, k4, v4, v4, v4)


def _rope_tables(n_ctx, t):
    half = HEAD // 2
    inv = 1.0 / (ROPE_BASE ** (jnp.arange(0, half, 2, dtype=F32) / half))
    pos = jnp.arange(t)
    ang_r = (pos // GRID_W).astype(F32)[:, None] * inv
    ang_c = (pos % GRID_W).astype(F32)[:, None] * inv
    cos = jnp.concatenate([jnp.cos(ang_r)] * 2 + [jnp.cos(ang_c)] * 2, axis=1)
    sin = jnp.concatenate([-jnp.sin(ang_r), jnp.sin(ang_r), -jnp.sin(ang_c), jnp.sin(ang_c)], axis=1)
    cos = jnp.concatenate([jnp.ones((n_ctx, HEAD), F32), cos], axis=0)
    sin = jnp.concatenate([jnp.zeros((n_ctx, HEAD), F32), sin], axis=0)
    return jnp.tile(cos, (1, 2)), jnp.tile(sin, (1, 2))


def kernel(x, c, ctx, c_ctx, ada_w, ada_b, ln1_g, ln1_b, ln2_g, ln2_b, rw_mu, rw_wr, rw_wk, rw_wv, rw_w0, rw_w1, rw_w2, rw_a0, rw_a1, rw_a2, rw_g1, rw_g2, rw_kk, rw_ka, rw_rk, rw_lnx_g, rw_lnx_b, rw_wo, at_wqkv, at_wo, at_sink, router_w, moe_w_gate, moe_w_up, moe_w_down):
    b, t, d = x.shape
    n_ctx = ctx.shape[1]
    depth = ada_w.shape[0]
    alpha = (2 * depth) ** 0.25
    assert n_ctx % ROW_TILE == 0 and t % ROW_TILE == 0 and b + 1 <= 8
    nctx_blk = n_ctx // ROW_TILE

    cc = jnp.concatenate([c, c_ctx[None], jnp.zeros((8 - b - 1, d), F32)], axis=0)
    mod_all = _ada_mod(cc, ada_w, ada_b).reshape(depth, 8, N_MOD, d)
    xa = jnp.concatenate([ctx, x], axis=1)
    bf = lambda w: w.astype(BF16)
    pad_rows = lambda w, lo: jnp.pad(w, ((lo, LANES - lo - w.shape[0]), (0, 0)))

    for i in range(depth):
        last = i == depth - 1
        j = i // 2
        mods = jnp.stack([jnp.broadcast_to(mod_all[i, b], (b, N_MOD, d)), mod_all[i, :b]], axis=1).reshape(2 * b, N_MOD, d)
        lnv1 = jnp.stack([ln1_g[i], ln1_b[i], jnp.full((d,), alpha, F32)])
        lnv2 = jnp.stack([ln2_g[i], ln2_b[i], jnp.full((d,), alpha, F32)])
        rw_pad = jnp.pad(router_w[i], ((0, 0), (0, LANES - N_EXPERTS)))
        if i % 2 == 0:
            vec = jnp.concatenate([rw_mu[j], rw_w0[j], rw_a0[j], rw_kk[j][None], rw_ka[j][None],
                                   rw_rk[j].reshape(1, d), jnp.zeros((3, d), F32)], axis=0)
            w1 = bf(jnp.concatenate([rw_w1[j, 0], rw_w1[j, 1]], axis=1))
            a1 = bf(jnp.concatenate([rw_a1[j, 0], rw_a1[j, 1]], axis=1))
            w2 = bf(jnp.stack([pad_rows(rw_w2[j, 0], 0), pad_rows(rw_w2[j, 1], HEAD)]))
            a2 = bf(jnp.stack([pad_rows(rw_a2[j, 0], 0), pad_rows(rw_a2[j, 1], HEAD)]))
            (r, v, kk, g, bonus, lw0, k0, e0, lw1, k1, e1) = _rwkv_proj(
                xa, mods, nctx_blk, vec, bf(rw_wr[j]), bf(rw_wk[j]), bf(rw_wv[j]), w1, a1, bf(rw_g1[j]),
                w2, a2, bf(rw_g2[j]))
            yf, yr = _rwkv_scan(r, v, kk, lw0, k0, e0, lw1, k1, e1, n_ctx // CHUNK)
            lnx = jnp.stack([rw_lnx_g[j], rw_lnx_b[j]])
            x1, h2, afft = _rwkv_out(yf, yr, bonus, g, xa, mods, nctx_blk, lnx, bf(rw_wo[j]), lnv1, rw_pad)
        else:
            nq = ATT_KV_HEADS * ATT_GROUP * HEAD
            nk = ATT_KV_HEADS * HEAD
            wqkv = at_wqkv[j]
            tile4 = lambda w: jnp.tile(w.reshape(d, ATT_KV_HEADS, 1, HEAD), (1, 1, ATT_GROUP, 1)).reshape(d, nq)
            cos, sin = _rope_tables(n_ctx, t)
            q, k4, v4 = _qkv(xa, mods, nctx_blk, cos, sin, bf(wqkv[:, :nq]),
                             bf(tile4(wqkv[:, nq:nq + nk])), bf(tile4(wqkv[:, nq + nk:])))
            o = _attention(q, k4, v4, at_sink[j], n_ctx)
            x1, h2, afft = _attn_out(o, xa, mods, nctx_blk, bf(at_wo[j]), lnv1, rw_pad)
        if last:
            if x1.shape[1] != t:
                x1, h2, afft = x1[:, n_ctx:], h2[:, n_ctx:], afft[:, :, n_ctx:]
            moe = _ec_moe(h2, afft, [(0, t)], i, moe_w_gate, moe_w_up, moe_w_down)
            return _moe_ln(x1, moe, mods, 0, lnv2)
        moe = _ec_moe(h2, afft, [(n_ctx, t), (0, n_ctx)], i, moe_w_gate, moe_w_up, moe_w_down)
        xa = _moe_ln(x1, moe, mods, nctx_blk, lnv2)
    return xa[:, n_ctx:]
```

```python
import functools

import jax
import jax.numpy as jnp
from jax import lax
from jax.experimental import pallas as pl
from jax.experimental.pallas import tpu as pltpu

F32 = jnp.float32
BF16 = jnp.bfloat16
I32 = jnp.int32
HIGHEST = lax.Precision.HIGHEST

LANES = 128
LN_EPS = 1e-5
N_MOD = 6
HEAD = 64
RW_LNX_EPS = 64e-5
RW_DECAY_SCALE = 0.606531
ATT_KV_HEADS = 4
ATT_GROUP = 4
WINDOW = 128
GRID_W = 64
ROPE_BASE = 10000.0
NEG = -1e30
N_EXPERTS = 16
EC_CAPACITY = 2
CHUNK = 64
DMA_UNROLL = 8
ROW_TILE = 256
VMEM_LIMIT = 56 * 2 ** 20


def _cparams(*sem):
    return pltpu.CompilerParams(dimension_semantics=sem, vmem_limit_bytes=VMEM_LIMIT)


def _dot(a, b):
    return jnp.dot(a, b, preferred_element_type=F32)


def _dot_nt(a, b):
    return lax.dot_general(a, b, (((1,), (1,)), ((), ())), preferred_element_type=F32)


def _split(x):
    hi = x.astype(BF16)
    lo = (x - hi.astype(F32)).astype(BF16)
    return hi, lo


def _head_sum(x):
    d = x.shape[-1]
    gr = (lax.broadcasted_iota(I32, (d, LANES), 0) // HEAD
          == lax.broadcasted_iota(I32, (d, LANES), 1)).astype(BF16)
    ge = (lax.broadcasted_iota(I32, (LANES, d), 0)
          == lax.broadcasted_iota(I32, (LANES, d), 1) // HEAD).astype(BF16)
    hi, lo = _split(x)
    s = _dot(hi, gr) + _dot(lo, gr)
    shi, slo = _split(s)
    return _dot(shi, ge) + _dot(slo, ge)


def _layer_norm(x, g, b):
    mu = jnp.mean(x, axis=-1, keepdims=True)
    xc = x - mu
    var = jnp.mean(xc * xc, axis=-1, keepdims=True)
    return xc * lax.rsqrt(var + LN_EPS) * g + b


def _ada_kernel(cc_ref, w_ref, b_ref, o_ref):
    cc = cc_ref[...]
    s = cc * jax.nn.sigmoid(cc)
    o_ref[0] = jnp.dot(s, w_ref[0], precision=HIGHEST, preferred_element_type=F32) + b_ref[0]


def _ada_mod(cc, ada_w, ada_b):
    depth, d, n = ada_w.shape
    tn = n // 4
    return pl.pallas_call(
        _ada_kernel,
        grid=(depth, n // tn),
        in_specs=[pl.BlockSpec((8, d), lambda l, j: (0, 0)),
                  pl.BlockSpec((1, d, tn), lambda l, j: (l, 0, j)),
                  pl.BlockSpec((1, 1, tn), lambda l, j: (l, 0, j))],
        out_specs=pl.BlockSpec((1, 8, tn), lambda l, j: (l, 0, j)),
        out_shape=jax.ShapeDtypeStruct((depth, 8, n), F32),
        compiler_params=_cparams("arbitrary", "arbitrary"),
    )(cc, ada_w, ada_b.reshape(depth, 1, n))


def _rwkv_proj_kernel(nctx_blk, x_ref, xp_ref, xn_ref, mod_ref, vec_ref, wr_ref, wk_ref, wv_ref,
                      w1_ref, a1_ref, g1_ref, w2_ref, a2_ref, g2_ref,
                      r_ref, v_ref, kk_ref, g_ref, bonus_ref,
                      lw0_ref, k0_ref, e0_ref, lw1_ref, k1_ref, e1_ref):
    i = pl.program_id(1)
    nblk = pl.num_programs(1)
    tm = x_ref.shape[1]
    m = mod_ref[0]
    shift, scale = m[0:1], m[1:2]
    h = x_ref[0] * (1 + scale) + shift
    first = (i == 0) | (i == nctx_blk)
    last = (i == nctx_blk - 1) | (i == nblk - 1)
    hp = jnp.where(first, 0.0, xp_ref[0][7:8] * (1 + scale) + shift)
    hn = jnp.where(last, 0.0, xn_ref[0][0:1] * (1 + scale) + shift)
    rows = lax.broadcasted_iota(I32, (tm, 1), 0)
    prev = jnp.where(rows == 0, hp, pltpu.roll(h, 1, 0))
    nxt = jnp.where(rows == tm - 1, hn, pltpu.roll(h, tm - 1, 0))
    xx = 0.5 * (prev + nxt) - h
    vec = vec_ref[...]
    mix = lambda j: (h + xx * vec[j:j + 1]).astype(BF16)
    r = _dot(mix(0), wr_ref[...])
    k = _dot(mix(2), wk_ref[...])
    v = _dot(mix(3), wv_ref[...])
    tw = jnp.tanh(_dot(mix(1), w1_ref[...])).astype(BF16)
    ta = _dot(mix(4), a1_ref[...]).astype(BF16)
    g = _dot(jax.nn.sigmoid(_dot(mix(5), g1_ref[...])).astype(BF16), g2_ref[...])
    kk = k * vec[10:11]
    nrm = jnp.sqrt(_head_sum(kk * kk))
    kk = kk / jnp.maximum(nrm, 1e-12)
    bonus = _head_sum(r * k * vec[12:13]) * v
    r_ref[0] = r
    v_ref[0] = v
    kk_ref[0] = kk
    g_ref[0] = g
    bonus_ref[0] = bonus
    for d, (lw_ref, kd_ref, e_ref) in enumerate(((lw0_ref, k0_ref, e0_ref), (lw1_ref, k1_ref, e1_ref))):
        lw_ref[0] = -RW_DECAY_SCALE * jax.nn.sigmoid(vec[6 + d:7 + d] + _dot(tw, w2_ref[d]))
        eta = jax.nn.sigmoid(vec[8 + d:9 + d] + _dot(ta, a2_ref[d]))
        e_ref[0] = eta
        kd_ref[0] = k * (1 + (eta - 1) * vec[11:12])


def _rwkv_proj(xa, mods, nctx_blk, vec, wr, wk, wv, w1, a1, g1, w2, a2, g2):
    b, r, d = xa.shape
    tm = ROW_TILE
    nblk = r // tm
    hb = tm // 8
    row = pl.BlockSpec((1, tm, d), lambda bb, i: (bb, i, 0))
    full = lambda a: pl.BlockSpec(a.shape, lambda bb, i: (0,) * a.ndim)
    out = jax.ShapeDtypeStruct((b, r, d), F32)
    return pl.pallas_call(
        functools.partial(_rwkv_proj_kernel, nctx_blk),
        grid=(b, nblk),
        in_specs=[row,
                  pl.BlockSpec((1, 8, d), lambda bb, i: (bb, jnp.maximum(i * hb - 1, 0), 0)),
                  pl.BlockSpec((1, 8, d), lambda bb, i: (bb, jnp.minimum((i + 1) * hb, r // 8 - 1), 0)),
                  pl.BlockSpec((1, N_MOD, d), lambda bb, i: (bb * 2 + (i >= nctx_blk).astype(I32), 0, 0)),
                  full(vec), full(wr), full(wk), full(wv), full(w1), full(a1), full(g1),
                  full(w2), full(a2), full(g2)],
        out_specs=[row] * 11,
        out_shape=[out] * 11,
        compiler_params=_cparams("parallel", "arbitrary"),
    )(xa, xa, xa, mods, vec, wr, wk, wv, w1, a1, g1, w2, a2, g2)


def _scan_kernel(rf_ref, vf_ref, kkf_ref, lw0_ref, k0_ref, e0_ref,
                 rr_ref, vr_ref, kkr_ref, lw1_ref, k1_ref, e1_ref,
                 yf_ref, yr_ref, s_ref):
    c = CHUNK
    assert c == HEAD

    @pl.when(pl.program_id(1) == 0)
    def _():
        s_ref[...] = jnp.zeros_like(s_ref)

    ti = lax.broadcasted_iota(I32, (c, c), 0)
    tj = lax.broadcasted_iota(I32, (c, c), 1)
    tt = lax.broadcasted_iota(I32, (c, 2 * c), 0)
    ss = lax.broadcasted_iota(I32, (c, 2 * c), 1) % c
    m0 = lax.broadcasted_iota(I32, (1, 2 * HEAD), 1) < HEAD

    def stack(x):
        return jnp.concatenate([jnp.where(m0, x, 0.0), jnp.where(m0, 0.0, x)], axis=0)

    dirs = ((rf_ref, vf_ref, kkf_ref, lw0_ref, k0_ref, e0_ref, yf_ref),
            (rr_ref, vr_ref, kkr_ref, lw1_ref, k1_ref, e1_ref, yr_ref))
    n_pair = s_ref.shape[1]
    ch = []
    for d, (r_ref, v_ref, kk_ref, lw_ref, kd_ref, e_ref, y_ref) in enumerate(dirs):
        if d == 0:
            tri = (tj <= ti).astype(BF16)
            strict, incl = ss < tt, ss <= tt
        else:
            tri = (tj >= ti).astype(BF16)
            strict, incl = ss > tt, ss >= tt
        lw = lw_ref[0]
        lw_hi = lw.astype(BF16)
        lw_mid = (lw - lw_hi.astype(F32)).astype(BF16)
        lw_lo = (lw - lw_hi.astype(F32) - lw_mid.astype(F32)).astype(BF16)
        cs = _dot(jnp.concatenate([tri, tri, tri], axis=1), jnp.concatenate([lw_hi, lw_mid, lw_lo], axis=0))
        gam = jnp.exp(cs)
        gam_inv = jnp.exp(-cs)
        kk = kk_ref[0]
        a_t = -(kk * jnp.exp(cs - lw))
        b_t = kk * e_ref[0] * gam_inv
        k_t = kd_ref[0] * gam_inv
        r_t = r_ref[0] * gam
        v_all = v_ref[0]
        g_end = gam[c - 1:c] if d == 0 else gam[0:1]
        for p in range(n_pair):
            sl = slice(2 * HEAD * p, 2 * HEAD * (p + 1))
            ch.append(dict(
                d=d, p=p, sl=sl, y_ref=y_ref, strict=strict, incl=incl, v2=stack(v_all[:, sl]),
                ar=jnp.concatenate([a_t[:, sl], r_t[:, sl]], axis=0).astype(BF16),
                bk=jnp.concatenate([stack(b_t[:, sl]), stack(k_t[:, sl])], axis=0),
                ge=g_end[:, sl], s0=s_ref[d, p]))
    for q in ch:
        q["m1"] = _dot_nt(q["ar"], q["bk"].astype(BF16))
    for q in ch:
        q["ars"] = _dot_nt(q["ar"], q["s0"].astype(BF16))
    for q in ch:
        m1 = q.pop("m1")
        q["pw"] = jnp.where(q["strict"], m1[:c, :2 * c], 0.0)
        lak = jnp.where(q["strict"], m1[:c, 2 * c:], 0.0).astype(BF16)
        q["lr"] = jnp.concatenate([jnp.where(q["incl"], m1[c:, :2 * c], 0.0),
                                   jnp.where(q["incl"], m1[c:, 2 * c:], 0.0)], axis=1).astype(BF16)
        q["x"] = q["ars"][:c] + _dot(lak, q["v2"].astype(BF16))
    for it in range(6):
        for q in ch:
            pb = q["pw"].astype(BF16)
            if it < 5:
                px = _dot(pb, jnp.concatenate([stack(q["x"]), stack(q["pw"])], axis=1).astype(BF16))
                q["x"] = q["x"] + px[:, :2 * c]
                q["pw"] = px[:, 2 * c:]
            else:
                q["x"] = q["x"] + _dot(pb, stack(q["x"]).astype(BF16))
    for q in ch:
        q["uv"] = jnp.concatenate([stack(q["x"]), q["v2"]], axis=0)
        q["y_ref"][0, :, q["sl"]] = q["ars"][c:] + _dot(q["lr"], q["uv"].astype(BF16))
    for q in ch:
        s_ref[q["d"], q["p"]] = (q["s0"] * q["ge"]
                                 + _dot(q["uv"].T.astype(BF16), (q["bk"] * q["ge"]).astype(BF16)))


def _rwkv_scan(r, v, kk, lw0, k0, e0, lw1, k1, e1, nctx_chunks):
    b, rows, d = r.shape
    c = CHUNK
    nc = rows // c
    fwd = lambda bb, i: (bb, i, 0)
    rev = lambda bb, i: (bb, jnp.where(i < nctx_chunks, nctx_chunks - 1 - i, nc - 1 - (i - nctx_chunks)), 0)
    sf = pl.BlockSpec((1, c, d), fwd)
    sr = pl.BlockSpec((1, c, d), rev)
    out = jax.ShapeDtypeStruct((b, rows, d), F32)
    return pl.pallas_call(
        _scan_kernel,
        grid=(b, nc),
        in_specs=[sf] * 6 + [sr] * 6,
        out_specs=[sf, sr],
        out_shape=[out, out],
        scratch_shapes=[pltpu.VMEM((2, d // (2 * HEAD), 2 * HEAD, 2 * HEAD), F32)],
        compiler_params=_cparams("parallel", "arbitrary"),
    )(r, v, kk, lw0, k0, e0, r, v, kk, lw1, k1, e1)


def _post_mixer(z, x, m, wo_ref, lnv, rw_ref, x1_ref, h2_ref, afft_ref):
    d = x.shape[-1]
    o = _dot(z.astype(BF16), wo_ref[...])
    x1 = _layer_norm(lnv[2:3] * x + m[2:3] * o, lnv[0:1], lnv[1:2])
    h2 = x1 * (1 + m[4:5]) + m[3:4]
    logits = jnp.dot(h2, rw_ref[...], precision=HIGHEST, preferred_element_type=F32)
    lane = lax.broadcasted_iota(I32, logits.shape, 1)
    logits = jnp.where(lane < N_EXPERTS, logits, NEG)
    ex = jnp.exp(logits - jnp.max(logits, axis=-1, keepdims=True))
    aff = ex / jnp.sum(ex, axis=-1, keepdims=True)
    x1_ref[0] = x1
    h2_ref[0, :, :d] = h2
    h2_ref[0, :, d:] = aff
    afft_ref[0] = aff.T[:N_EXPERTS]


def _rwkv_out_kernel(yf_ref, yr_ref, bonus_ref, g_ref, x_ref, mod_ref, lnx_ref, wo_ref, lnv_ref, rw_ref,
                     x1_ref, h2_ref, afft_ref):
    y = yf_ref[0] + yr_ref[0]
    mu = _head_sum(y) * (1.0 / HEAD)
    yc = y - mu
    var = _head_sum(yc * yc) * (1.0 / HEAD)
    lnx = lnx_ref[...]
    yn = yc * lax.rsqrt(var + RW_LNX_EPS) * lnx[0:1] + lnx[1:2]
    z = (yn + bonus_ref[0]) * g_ref[0]
    _post_mixer(z, x_ref[0], mod_ref[0], wo_ref, lnv_ref[...], rw_ref, x1_ref, h2_ref, afft_ref)


def _post_mixer_outs(b, rows, d, tm):
    specs = [pl.BlockSpec((1, tm, d), lambda bb, i: (bb, i, 0)),
             pl.BlockSpec((1, tm, d + LANES), lambda bb, i: (bb, i, 0)),
             pl.BlockSpec((1, N_EXPERTS, tm), lambda bb, i: (bb, 0, i))]
    shapes = [jax.ShapeDtypeStruct((b, rows, d), F32), jax.ShapeDtypeStruct((b, rows, d + LANES), F32),
              jax.ShapeDtypeStruct((b, N_EXPERTS, rows), F32)]
    return specs, shapes


def _rwkv_out(yf, yr, bonus, g, xa, mods, nctx_blk, lnx, wo, lnv, rw_pad):
    b, r, d = xa.shape
    tm = ROW_TILE
    row = pl.BlockSpec((1, tm, d), lambda bb, i: (bb, i, 0))
    full = lambda a: pl.BlockSpec(a.shape, lambda bb, i: (0,) * a.ndim)
    out_specs, out_shape = _post_mixer_outs(b, r, d, tm)
    return pl.pallas_call(
        _rwkv_out_kernel,
        grid=(b, r // tm),
        in_specs=[row] * 5 + [pl.BlockSpec((1, N_MOD, d), lambda bb, i: (bb * 2 + (i >= nctx_blk).astype(I32), 0, 0)),
                              full(lnx), full(wo), full(lnv), full(rw_pad)],
        out_specs=out_specs,
        out_shape=out_shape,
        compiler_params=_cparams("parallel", "arbitrary"),
    )(yf, yr, bonus, g, xa, mods, lnx, wo, lnv, rw_pad)


def _attn_out_kernel(o_ref, x_ref, mod_ref, wo_ref, lnv_ref, rw_ref, x1_ref, h2_ref, afft_ref):
    _post_mixer(o_ref[0], x_ref[0], mod_ref[0], wo_ref, lnv_ref[...], rw_ref, x1_ref, h2_ref, afft_ref)


def _attn_out(o, xa, mods, blk_off, wo, lnv, rw_pad):
    b, t, d = o.shape
    tm = ROW_TILE
    row = pl.BlockSpec((1, tm, d), lambda bb, i: (bb, i, 0))
    full = lambda a: pl.BlockSpec(a.shape, lambda bb, i: (0,) * a.ndim)
    out_specs, out_shape = _post_mixer_outs(b, t, d, tm)
    return pl.pallas_call(
        _attn_out_kernel,
        grid=(b, t // tm),
        in_specs=[row, pl.BlockSpec((1, tm, d), lambda bb, i: (bb, i + blk_off, 0)),
                  pl.BlockSpec((1, N_MOD, d), lambda bb, i: (bb * 2 + 1, 0, 0)),
                  full(wo), full(lnv), full(rw_pad)],
        out_specs=out_specs,
        out_shape=out_shape,
        compiler_params=_cparams("parallel", "arbitrary"),
    )(o, xa, mods, wo, lnv, rw_pad)


def _select_kernel(cap, aff_ref, idx_ref, c_ref):
    n_e, tn = aff_ref.shape[1], aff_ref.shape[2]

    def bit_step(j, bits):
        cand = bits | jnp.left_shift(jnp.int32(1), 30 - j)
        cnt = jnp.sum((aff_ref[0] >= lax.bitcast_convert_type(cand, F32)).astype(F32), axis=1, keepdims=True)
        return jnp.where(cnt >= cap, cand, bits)

    bits = lax.fori_loop(0, 31, bit_step, jnp.zeros((n_e, 1), I32))
    thr = lax.bitcast_convert_type(bits, F32)
    need = cap - jnp.sum((aff_ref[0] > thr).astype(F32), axis=1, keepdims=True)
    triu = (lax.broadcasted_iota(I32, (LANES, LANES), 0) <= lax.broadcasted_iota(I32, (LANES, LANES), 1)).astype(BF16)
    tie_off = jnp.zeros((n_e, 1), F32)
    sel_off = jnp.zeros((n_e, 1), F32)
    for j in range(tn // LANES):
        cols = slice(j * LANES, (j + 1) * LANES)
        a = aff_ref[0, :, cols]
        eq = (a == thr).astype(F32)
        tie_rank = _dot(eq.astype(BF16), triu) + tie_off - eq
        sel = ((a > thr) | ((eq > 0) & (tie_rank < need))).astype(F32)
        c_ref[:, cols] = _dot(sel.astype(BF16), triu) + sel_off
        tie_off = tie_off + jnp.sum(eq, axis=1, keepdims=True)
        sel_off = sel_off + jnp.sum(sel, axis=1, keepdims=True)

    ns = min(cap, 256)
    tc = min(tn, 1024)
    lane = lax.broadcasted_iota(I32, (1, LANES), 1)
    for ci in range(cap // ns):
        slot = (lax.broadcasted_iota(I32, (ns, 1), 0) + ci * ns).astype(F32)

        def per_expert(e, out):
            def chunk(j, part):
                ce = c_ref[pl.ds(e, 1), pl.ds(pl.multiple_of(j * tc, tc), tc)]
                for k in range(tc // LANES):
                    part = part + (ce[:, k * LANES:(k + 1) * LANES] <= slot).astype(F32)
                return part

            part = lax.fori_loop(0, tn // tc, chunk, jnp.zeros((ns, LANES), F32))
            return jnp.where(lane == e, jnp.sum(part, axis=1, keepdims=True), out)

        out = lax.fori_loop(0, n_e, per_expert, jnp.zeros((ns, LANES), F32))
        idx_ref[0, ci * ns:(ci + 1) * ns, :] = out.astype(I32)


def _select(afft, cap):
    b, n_e, tn = afft.shape
    return pl.pallas_call(
        functools.partial(_select_kernel, cap),
        grid=(b,),
        in_specs=[pl.BlockSpec((1, n_e, tn), lambda bb: (bb, 0, 0))],
        out_specs=pl.BlockSpec((1, cap, LANES), lambda bb: (bb, 0, 0)),
        out_shape=jax.ShapeDtypeStruct((b, cap, LANES), I32),
        scratch_shapes=[pltpu.VMEM((n_e, tn), F32)],
        compiler_params=_cparams("parallel"),
    )(afft)


def _expert_kernel(idx_ref, h_hbm, wg_ref, wu_ref, wd_ref, acc_in, acc_hbm,
                   stage, accst, xin, gate, yacc, sem_x, sem_a, sem_s):
    del acc_in
    e = pl.program_id(0)
    f = pl.program_id(1)
    n_e = pl.num_programs(0)
    nf, mb, dx = stage.shape
    m, d = xin.shape
    base = e * m
    nxt = lax.rem(e + 1, n_e) * m

    def x_copy(first, blk, u):
        return pltpu.make_async_copy(h_hbm.at[pl.ds(idx_ref[first + blk * mb + u], 1)],
                                     stage.at[blk, pl.ds(u, 1)], sem_x)

    def a_copy(blk, u):
        return pltpu.make_async_copy(acc_hbm.at[pl.ds(idx_ref[base + blk * mb + u], 1)],
                                     accst.at[blk, pl.ds(u, 1)], sem_a)

    def s_copy(blk, u):
        return pltpu.make_async_copy(accst.at[blk, pl.ds(u, 1)],
                                     acc_hbm.at[pl.ds(idx_ref[base + blk * mb + u], 1)], sem_s)

    def looped(copy_of):
        for blk in range(nf):
            def body(i, carry):
                for u in range(DMA_UNROLL):
                    copy_of(blk, i * DMA_UNROLL + u).start()
                return carry

            lax.fori_loop(0, mb // DMA_UNROLL, body, 0)

    def wait_blocks(src_of, dst_of, sem):
        for blk in range(nf):
            pltpu.make_async_copy(src_of(blk), dst_of(blk), sem).wait()

    wait_x = lambda: wait_blocks(lambda blk: h_hbm.at[pl.ds(0, mb)], lambda blk: stage.at[blk], sem_x)

    @pl.when((e == 0) & (f == 0))
    def _():
        looped(lambda blk, u: x_copy(base, blk, u))
        wait_x()

    @pl.when(f == 0)
    def _():
        rows = stage[...]
        xin[...] = rows[:, :, :d].reshape(m, d).astype(BF16)
        lane = lax.broadcasted_iota(I32, (1, 1, LANES), 2)
        gate[...] = jnp.sum(jnp.where(lane == e, rows[:, :, d:], 0.0), axis=2, keepdims=True).reshape(m, 1)
        yacc[...] = jnp.zeros_like(yacc)

    for u in range(mb):
        x_copy(nxt, f, u).start()
        a_copy(f, u).start()

    x = xin[...]
    a = _dot(x, wg_ref[0, 0].astype(BF16))
    u = _dot(x, wu_ref[0, 0].astype(BF16))
    hm = (a * jax.nn.sigmoid(a) * u).astype(BF16)
    wd = wd_ref[0, 0].astype(BF16)
    rb = m // 4
    for r0 in range(0, m, rb):
        yacc[r0:r0 + rb] += _dot(hm[r0:r0 + rb], wd)

    @pl.when(f == nf - 1)
    def _():
        wait_blocks(lambda blk: acc_hbm.at[pl.ds(0, mb)], lambda blk: accst.at[blk], sem_a)
        accst[...] = accst[...] + (gate[...] * yacc[...]).reshape(nf, mb, d)
        looped(s_copy)
        wait_blocks(lambda blk: accst.at[blk], lambda blk: acc_hbm.at[pl.ds(0, mb)], sem_s)
        wait_x()


def _experts(idx_flat, m, h_rows, acc, layer, w_gate, w_up, w_down):
    _, n_e, d, f = w_gate.shape
    tf = 256
    nf = f // tf
    mb = m // nf
    assert mb * nf == m and mb % DMA_UNROLL == 0 and m % 64 == 0 and h_rows.shape[1] == d + LANES
    grid_spec = pltpu.PrefetchScalarGridSpec(
        num_scalar_prefetch=1,
        grid=(n_e, f // tf),
        in_specs=[pl.BlockSpec(memory_space=pl.ANY),
                  pl.BlockSpec((1, 1, d, tf), lambda e, j, idx: (layer, e, 0, j)),
                  pl.BlockSpec((1, 1, d, tf), lambda e, j, idx: (layer, e, 0, j)),
                  pl.BlockSpec((1, 1, tf, d), lambda e, j, idx: (layer, e, j, 0)),
                  pl.BlockSpec(memory_space=pl.ANY)],
        out_specs=pl.BlockSpec(memory_space=pl.ANY),
        scratch_shapes=[pltpu.VMEM((nf, mb, d + LANES), F32), pltpu.VMEM((nf, mb, d), F32),
                        pltpu.VMEM((m, d), BF16), pltpu.VMEM((m, 1), F32), pltpu.VMEM((m, d), F32),
                        pltpu.SemaphoreType.DMA(()), pltpu.SemaphoreType.DMA(()), pltpu.SemaphoreType.DMA(())],
    )
    return pl.pallas_call(
        _expert_kernel,
        grid_spec=grid_spec,
        out_shape=jax.ShapeDtypeStruct(acc.shape, F32),
        input_output_aliases={5: 0},
        compiler_params=_cparams("arbitrary", "arbitrary"),
    )(idx_flat, h_rows, w_gate, w_up, w_down, acc)


def _moe_ln_kernel(x_ref, moe_ref, mod_ref, lnv_ref, o_ref):
    lnv = lnv_ref[...]
    m = mod_ref[0]
    o_ref[0] = _layer_norm(lnv[2:3] * x_ref[0] + m[5:6] * moe_ref[0], lnv[0:1], lnv[1:2])


def _moe_ln(x1, moe, mods, nctx_blk, lnv):
    b, r, d = x1.shape
    tm = ROW_TILE
    row = pl.BlockSpec((1, tm, d), lambda bb, i: (bb, i, 0))
    return pl.pallas_call(
        _moe_ln_kernel,
        grid=(b, r // tm),
        in_specs=[row, row,
                  pl.BlockSpec((1, N_MOD, d), lambda bb, i: (bb * 2 + (i >= nctx_blk).astype(I32), 0, 0)),
                  pl.BlockSpec(lnv.shape, lambda bb, i: (0, 0))],
        out_specs=row,
        out_shape=jax.ShapeDtypeStruct((b, r, d), F32),
        compiler_params=_cparams("parallel", "arbitrary"),
    )(x1, moe, mods, lnv)


def _ec_moe(h2x, afft, streams, layer, w_gate, w_up, w_down):
    b, r, dx = h2x.shape
    d = dx - LANES
    idx_parts = []
    for off, tn in streams:
        cap = EC_CAPACITY * tn // N_EXPERTS
        idx = _select(lax.slice_in_dim(afft, off, off + tn, axis=2), cap)[:, :, :N_EXPERTS]
        rows = idx + (jnp.arange(b, dtype=I32) * r + off)[:, None, None]
        idx_parts.append(jnp.transpose(rows, (2, 0, 1)).reshape(N_EXPERTS, b * cap))
    idx_flat = jnp.concatenate(idx_parts, axis=1)
    acc = _experts(idx_flat.reshape(-1), idx_flat.shape[1], h2x.reshape(b * r, dx),
                   jnp.zeros((b * r, d), F32), layer, w_gate, w_up, w_down)
    return acc.reshape(b, r, d)


def _qkv_kernel(x_ref, mod_ref, cos_ref, sin_ref, wq_ref, wk_ref, wv_ref, q_ref, k_ref, v_ref):
    m = mod_ref[0]
    h = (x_ref[0] * (1 + m[1:2]) + m[0:1]).astype(BF16)
    d = x_ref.shape[2]
    rep = d // LANES
    cos = jnp.concatenate([cos_ref[...]] * rep, axis=1)
    sin = jnp.concatenate([sin_ref[...]] * rep, axis=1)
    lane = lax.broadcasted_iota(I32, (1, d), 1)
    lower = (lane % 32) < 16

    def rope(t):
        partner = jnp.where(lower, pltpu.roll(t, d - 16, 1), pltpu.roll(t, 16, 1))
        return t * cos + partner * sin

    q_ref[0] = (rope(_dot(h, wq_ref[...])) * (HEAD ** -0.5)).astype(BF16)
    k_ref[0] = rope(_dot(h, wk_ref[...])).astype(BF16)
    v_ref[0] = _dot(h, wv_ref[...]).astype(BF16)


def _qkv(xa, mods, nctx_blk, cos, sin, wq, wk4, wv4):
    b, r, d = xa.shape
    tm = ROW_TILE
    row = pl.BlockSpec((1, tm, d), lambda bb, i: (bb, i, 0))
    tab = pl.BlockSpec((tm, LANES), lambda bb, i: (i, 0))
    full = lambda a: pl.BlockSpec(a.shape, lambda bb, i: (0,) * a.ndim)
    out = jax.ShapeDtypeStruct((b, r, d), BF16)
    return pl.pallas_call(
        _qkv_kernel,
        grid=(b, r // tm),
        in_specs=[row, pl.BlockSpec((1, N_MOD, d), lambda bb, i: (bb * 2 + (i >= nctx_blk).astype(I32), 0, 0)),
                  tab, tab, full(wq), full(wk4), full(wv4)],
        out_specs=[row] * 3,
        out_shape=[out] * 3,
        compiler_params=_cparams("parallel", "arbitrary"),
    )(xa, mods, cos, sin, wq, wk4, wv4)


def _attn_kernel(n_ctx, sink_ref, q_ref, kc_ref, vc_ref, kp_ref, k0_ref, kn_ref, vp_ref, v0_ref, vn_ref, o_ref):
    nb = pl.program_id(1)
    n_blk = pl.num_programs(1)
    bq = q_ref.shape[1]
    gw = ATT_GROUP * HEAD
    nk = n_ctx + 3 * bq
    col = lax.broadcasted_iota(I32, (bq, nk), 1)
    qpos = nb * bq + lax.broadcasted_iota(I32, (bq, nk), 0)
    kpos = (nb - 1) * bq + (col - n_ctx)
    valid = (col < n_ctx) | ((kpos >= 0) & (kpos < n_blk * bq) & (jnp.abs(qpos - kpos) <= WINDOW))
    lane = lax.broadcasted_iota(I32, (1, gw), 1)
    heads = []
    for h in range(ATT_KV_HEADS):
        sl = slice(gw * h, gw * (h + 1))
        keys = jnp.concatenate([kc_ref[0, :, sl], kp_ref[0, :, sl], k0_ref[0, :, sl], kn_ref[0, :, sl]], axis=0)
        vals = jnp.concatenate([vc_ref[0, :, sl], vp_ref[0, :, sl], v0_ref[0, :, sl], vn_ref[0, :, sl]], axis=0)
        qh = q_ref[0, :, sl]
        scores = [_dot_nt(jnp.where((lane // HEAD) == g, qh, jnp.zeros_like(qh)), keys)
                  for g in range(ATT_GROUP)]
        heads.append((sl, vals, scores))
    probs = []
    for h, (sl, vals, scores) in enumerate(heads):
        row = []
        for g, s in enumerate(scores):
            s = jnp.where(valid, s, NEG)
            sink = sink_ref[h * ATT_GROUP + g]
            mx = jnp.maximum(jnp.max(s, axis=-1, keepdims=True), sink)
            p = jnp.exp(s - mx)
            den = jnp.sum(p, axis=-1, keepdims=True) + jnp.exp(sink - mx)
            row.append((p / den).astype(BF16))
        probs.append(row)
    for (sl, vals, _), row in zip(heads, probs):
        acc = jnp.zeros((bq, gw), F32)
        for g, p in enumerate(row):
            acc = acc + jnp.where((lane // HEAD) == g, _dot(p, vals), 0.0)
        o_ref[0, :, sl] = acc.astype(BF16)


def _attention(q, k4, v4, sink, n_ctx):
    b, r, d = q.shape
    bq = WINDOW
    t = r - n_ctx
    nb = t // bq
    off = n_ctx // bq
    blk = lambda f: pl.BlockSpec((1, bq, d), f)
    ctx_spec = pl.BlockSpec((1, n_ctx, d), lambda bb, i: (bb, 0, 0))
    prev = lambda bb, i: (bb, jnp.maximum(i - 1, 0) + off, 0)
    cur = lambda bb, i: (bb, i + off, 0)
    nxt = lambda bb, i: (bb, jnp.minimum(i + 1, nb - 1) + off, 0)
    return pl.pallas_call(
        functools.partial(_attn_kernel, n_ctx),
        grid=(b, nb),
        in_specs=[pl.BlockSpec(memory_space=pltpu.SMEM), blk(cur), ctx_spec, ctx_spec,
                  blk(prev), blk(cur), blk(nxt), blk(prev), blk(cur), blk(nxt)],
        out_specs=pl.BlockSpec((1, bq, d), lambda bb, i: (bb, i, 0)),
        out_shape=jax.ShapeDtypeStruct((b, t, d), BF16),
        compiler_params=_cparams("parallel", "arbitrary"),
    )(sink, q, k4, v4, k4, k4, k4, v4, v4, v4)


def _rope_tables(n_ctx, t):
    half = HEAD // 2
    inv = 1.0 / (ROPE_BASE ** (jnp.arange(0, half, 2, dtype=F32) / half))
    pos = jnp.arange(t)
    ang_r = (pos // GRID_W).astype(F32)[:, None] * inv
    ang_c = (pos % GRID_W).astype(F32)[:, None] * inv
    cos = jnp.concatenate([jnp.cos(ang_r)] * 2 + [jnp.cos(ang_c)] * 2, axis=1)
    sin = jnp.concatenate([-jnp.sin(ang_r), jnp.sin(ang_r), -jnp.sin(ang_c), jnp.sin(ang_c)], axis=1)
    cos = jnp.concatenate([jnp.ones((n_ctx, HEAD), F32), cos], axis=0)
    sin = jnp.concatenate([jnp.zeros((n_ctx, HEAD), F32), sin], axis=0)
    return jnp.tile(cos, (1, 2)), jnp.tile(sin, (1, 2))


def kernel(x, c, ctx, c_ctx, ada_w, ada_b, ln1_g, ln1_b, ln2_g, ln2_b, rw_mu, rw_wr, rw_wk, rw_wv, rw_w0, rw_w1, rw_w2, rw_a0, rw_a1, rw_a2, rw_g1, rw_g2, rw_kk, rw_ka, rw_rk, rw_lnx_g, rw_lnx_b, rw_wo, at_wqkv, at_wo, at_sink, router_w, moe_w_gate, moe_w_up, moe_w_down):
    b, t, d = x.shape
    n_ctx = ctx.shape[1]
    depth = ada_w.shape[0]
    alpha = (2 * depth) ** 0.25
    assert n_ctx % ROW_TILE == 0 and t % ROW_TILE == 0 and b + 1 <= 8
    nctx_blk = n_ctx // ROW_TILE

    cc = jnp.concatenate([c, c_ctx[None], jnp.zeros((8 - b - 1, d), F32)], axis=0)
    mod_all = _ada_mod(cc, ada_w, ada_b).reshape(depth, 8, N_MOD, d)
    xa = jnp.concatenate([ctx, x], axis=1)
    bf = lambda w: w.astype(BF16)
    pad_rows = lambda w, lo: jnp.pad(w, ((lo, LANES - lo - w.shape[0]), (0, 0)))

    for i in range(depth):
        last = i == depth - 1
        j = i // 2
        mods = jnp.stack([jnp.broadcast_to(mod_all[i, b], (b, N_MOD, d)), mod_all[i, :b]], axis=1).reshape(2 * b, N_MOD, d)
        lnv1 = jnp.stack([ln1_g[i], ln1_b[i], jnp.full((d,), alpha, F32)])
        lnv2 = jnp.stack([ln2_g[i], ln2_b[i], jnp.full((d,), alpha, F32)])
        rw_pad = jnp.pad(router_w[i], ((0, 0), (0, LANES - N_EXPERTS)))
        if i % 2 == 0:
            vec = jnp.concatenate([rw_mu[j], rw_w0[j], rw_a0[j], rw_kk[j][None], rw_ka[j][None],
                                   rw_rk[j].reshape(1, d), jnp.zeros((3, d), F32)], axis=0)
            w1 = bf(jnp.concatenate([rw_w1[j, 0], rw_w1[j, 1]], axis=1))
            a1 = bf(jnp.concatenate([rw_a1[j, 0], rw_a1[j, 1]], axis=1))
            w2 = bf(jnp.stack([pad_rows(rw_w2[j, 0], 0), pad_rows(rw_w2[j, 1], HEAD)]))
            a2 = bf(jnp.stack([pad_rows(rw_a2[j, 0], 0), pad_rows(rw_a2[j, 1], HEAD)]))
            (r, v, kk, g, bonus, lw0, k0, e0, lw1, k1, e1) = _rwkv_proj(
                xa, mods, nctx_blk, vec, bf(rw_wr[j]), bf(rw_wk[j]), bf(rw_wv[j]), w1, a1, bf(rw_g1[j]),
                w2, a2, bf(rw_g2[j]))
            yf, yr = _rwkv_scan(r, v, kk, lw0, k0, e0, lw1, k1, e1, n_ctx // CHUNK)
            lnx = jnp.stack([rw_lnx_g[j], rw_lnx_b[j]])
            x1, h2, afft = _rwkv_out(yf, yr, bonus, g, xa, mods, nctx_blk, lnx, bf(rw_wo[j]), lnv1, rw_pad)
        else:
            nq = ATT_KV_HEADS * ATT_GROUP * HEAD
            nk = ATT_KV_HEADS * HEAD
            wqkv = at_wqkv[j]
            tile4 = lambda w: jnp.tile(w.reshape(d, ATT_KV_HEADS, 1, HEAD), (1, 1, ATT_GROUP, 1)).reshape(d, nq)
            cos, sin = _rope_tables(n_ctx, t)
            q, k4, v4 = _qkv(xa, mods, nctx_blk, cos, sin, bf(wqkv[:, :nq]),
                             bf(tile4(wqkv[:, nq:nq + nk])), bf(tile4(wqkv[:, nq + nk:])))
            o = _attention(q, k4, v4, at_sink[j], n_ctx)
            x1, h2, afft = _attn_out(o, xa, mods, nctx_blk, bf(at_wo[j]), lnv1, rw_pad)
        if last:
            if x1.shape[1] != t:
                x1, h2, afft = x1[:, n_ctx:], h2[:, n_ctx:], afft[:, :, n_ctx:]
            moe = _ec_moe(h2, afft, [(0, t)], i, moe_w_gate, moe_w_up, moe_w_down)
            return _moe_ln(x1, moe, mods, 0, lnv2)
        moe = _ec_moe(h2, afft, [(n_ctx, t), (0, n_ctx)], i, moe_w_gate, moe_w_up, moe_w_down)
        xa = _moe_ln(x1, moe, mods, nctx_blk, lnv2)
    return xa[:, n_ctx:]
```

```python
import functools

import jax
import jax.numpy as jnp
from jax import lax
from jax.experimental import pallas as pl
from jax.experimental.pallas import tpu as pltpu

F32 = jnp.float32
BF16 = jnp.bfloat16
I32 = jnp.int32
HIGHEST = lax.Precision.HIGHEST

LANES = 128
LN_EPS = 1e-5
N_MOD = 6
HEAD = 64
RW_LNX_EPS = 64e-5
RW_DECAY_SCALE = 0.606531
ATT_KV_HEADS = 4
ATT_GROUP = 4
WINDOW = 128
GRID_W = 64
ROPE_BASE = 10000.0
NEG = -1e30
N_EXPERTS = 16
EC_CAPACITY = 2
CHUNK = 64
DMA_UNROLL = 8
EXPERT_F_TILE = 256
EXPERT_SCATTER_STEPS = 2
EXPERT_GATHER_STEPS = 4
ROW_TILE = 256
VMEM_LIMIT = 56 * 2 ** 20


def _cparams(*sem):
    return pltpu.CompilerParams(dimension_semantics=sem, vmem_limit_bytes=VMEM_LIMIT)


def _dot(a, b):
    return jnp.dot(a, b, preferred_element_type=F32)


def _dot_nt(a, b):
    return lax.dot_general(a, b, (((1,), (1,)), ((), ())), preferred_element_type=F32)


def _split(x):
    hi = x.astype(BF16)
    lo = (x - hi.astype(F32)).astype(BF16)
    return hi, lo


def _head_sum(x):
    d = x.shape[-1]
    gr = (lax.broadcasted_iota(I32, (d, LANES), 0) // HEAD
          == lax.broadcasted_iota(I32, (d, LANES), 1)).astype(BF16)
    ge = (lax.broadcasted_iota(I32, (LANES, d), 0)
          == lax.broadcasted_iota(I32, (LANES, d), 1) // HEAD).astype(BF16)
    hi, lo = _split(x)
    s = _dot(hi, gr) + _dot(lo, gr)
    shi, slo = _split(s)
    return _dot(shi, ge) + _dot(slo, ge)


def _layer_norm(x, g, b):
    mu = jnp.mean(x, axis=-1, keepdims=True)
    xc = x - mu
    var = jnp.mean(xc * xc, axis=-1, keepdims=True)
    return xc * lax.rsqrt(var + LN_EPS) * g + b


def _ada_kernel(cc_ref, w_ref, b_ref, o_ref):
    cc = cc_ref[...]
    s = cc * jax.nn.sigmoid(cc)
    o_ref[0] = jnp.dot(s, w_ref[0], precision=HIGHEST, preferred_element_type=F32) + b_ref[0]


def _ada_mod(cc, ada_w, ada_b):
    depth, d, n = ada_w.shape
    tn = n // 4
    return pl.pallas_call(
        _ada_kernel,
        grid=(depth, n // tn),
        in_specs=[pl.BlockSpec((8, d), lambda l, j: (0, 0)),
                  pl.BlockSpec((1, d, tn), lambda l, j: (l, 0, j)),
                  pl.BlockSpec((1, 1, tn), lambda l, j: (l, 0, j))],
        out_specs=pl.BlockSpec((1, 8, tn), lambda l, j: (l, 0, j)),
        out_shape=jax.ShapeDtypeStruct((depth, 8, n), F32),
        compiler_params=_cparams("arbitrary", "arbitrary"),
    )(cc, ada_w, ada_b.reshape(depth, 1, n))


def _rwkv_proj_kernel(nctx_blk, x_ref, xp_ref, xn_ref, mod_ref, vec_ref, wr_ref, wk_ref, wv_ref,
                      w1_ref, a1_ref, g1_ref, w2_ref, a2_ref, g2_ref,
                      r_ref, v_ref, kk_ref, g_ref, bonus_ref,
                      lw0_ref, k0_ref, e0_ref, lw1_ref, k1_ref, e1_ref):
    i = pl.program_id(1)
    nblk = pl.num_programs(1)
    tm = x_ref.shape[1]
    m = mod_ref[0]
    shift, scale = m[0:1], m[1:2]
    h = x_ref[0] * (1 + scale) + shift
    first = (i == 0) | (i == nctx_blk)
    last = (i == nctx_blk - 1) | (i == nblk - 1)
    hp = jnp.where(first, 0.0, xp_ref[0][7:8] * (1 + scale) + shift)
    hn = jnp.where(last, 0.0, xn_ref[0][0:1] * (1 + scale) + shift)
    rows = lax.broadcasted_iota(I32, (tm, 1), 0)
    prev = jnp.where(rows == 0, hp, pltpu.roll(h, 1, 0))
    nxt = jnp.where(rows == tm - 1, hn, pltpu.roll(h, tm - 1, 0))
    xx = 0.5 * (prev + nxt) - h
    vec = vec_ref[...]
    mix = lambda j: (h + xx * vec[j:j + 1]).astype(BF16)
    r = _dot(mix(0), wr_ref[...])
    k = _dot(mix(2), wk_ref[...])
    v = _dot(mix(3), wv_ref[...])
    tw = jnp.tanh(_dot(mix(1), w1_ref[...])).astype(BF16)
    ta = _dot(mix(4), a1_ref[...]).astype(BF16)
    g = _dot(jax.nn.sigmoid(_dot(mix(5), g1_ref[...])).astype(BF16), g2_ref[...])
    kk = k * vec[10:11]
    nrm = jnp.sqrt(_head_sum(kk * kk))
    kk = kk / jnp.maximum(nrm, 1e-12)
    bonus = _head_sum(r * k * vec[12:13]) * v
    r_ref[0] = r
    v_ref[0] = v
    kk_ref[0] = kk
    g_ref[0] = g
    bonus_ref[0] = bonus
    for d, (lw_ref, kd_ref, e_ref) in enumerate(((lw0_ref, k0_ref, e0_ref), (lw1_ref, k1_ref, e1_ref))):
        lw_ref[0] = -RW_DECAY_SCALE * jax.nn.sigmoid(vec[6 + d:7 + d] + _dot(tw, w2_ref[d]))
        eta = jax.nn.sigmoid(vec[8 + d:9 + d] + _dot(ta, a2_ref[d]))
        e_ref[0] = eta
        kd_ref[0] = k * (1 + (eta - 1) * vec[11:12])


def _rwkv_proj(xa, mods, nctx_blk, vec, wr, wk, wv, w1, a1, g1, w2, a2, g2):
    b, r, d = xa.shape
    tm = ROW_TILE
    nblk = r // tm
    hb = tm // 8
    row = pl.BlockSpec((1, tm, d), lambda bb, i: (bb, i, 0))
    full = lambda a: pl.BlockSpec(a.shape, lambda bb, i: (0,) * a.ndim)
    out = jax.ShapeDtypeStruct((b, r, d), F32)
    return pl.pallas_call(
        functools.partial(_rwkv_proj_kernel, nctx_blk),
        grid=(b, nblk),
        in_specs=[row,
                  pl.BlockSpec((1, 8, d), lambda bb, i: (bb, jnp.maximum(i * hb - 1, 0), 0)),
                  pl.BlockSpec((1, 8, d), lambda bb, i: (bb, jnp.minimum((i + 1) * hb, r // 8 - 1), 0)),
                  pl.BlockSpec((1, N_MOD, d), lambda bb, i: (bb * 2 + (i >= nctx_blk).astype(I32), 0, 0)),
                  full(vec), full(wr), full(wk), full(wv), full(w1), full(a1), full(g1),
                  full(w2), full(a2), full(g2)],
        out_specs=[row] * 11,
        out_shape=[out] * 11,
        compiler_params=_cparams("parallel", "arbitrary"),
    )(xa, xa, xa, mods, vec, wr, wk, wv, w1, a1, g1, w2, a2, g2)


def _scan_kernel(rf_ref, vf_ref, kkf_ref, lw0_ref, k0_ref, e0_ref,
                 rr_ref, vr_ref, kkr_ref, lw1_ref, k1_ref, e1_ref,
                 yf_ref, yr_ref, s_ref):
    c = CHUNK
    assert c == HEAD

    @pl.when(pl.program_id(1) == 0)
    def _():
        s_ref[...] = jnp.zeros_like(s_ref)

    ti = lax.broadcasted_iota(I32, (c, c), 0)
    tj = lax.broadcasted_iota(I32, (c, c), 1)
    tt = lax.broadcasted_iota(I32, (c, 2 * c), 0)
    ss = lax.broadcasted_iota(I32, (c, 2 * c), 1) % c
    m0 = lax.broadcasted_iota(I32, (1, 2 * HEAD), 1) < HEAD

    def stack(x):
        return jnp.concatenate([jnp.where(m0, x, 0.0), jnp.where(m0, 0.0, x)], axis=0)

    dirs = ((rf_ref, vf_ref, kkf_ref, lw0_ref, k0_ref, e0_ref, yf_ref),
            (rr_ref, vr_ref, kkr_ref, lw1_ref, k1_ref, e1_ref, yr_ref))
    n_pair = s_ref.shape[1]
    ch = []
    for d, (r_ref, v_ref, kk_ref, lw_ref, kd_ref, e_ref, y_ref) in enumerate(dirs):
        if d == 0:
            tri = (tj <= ti).astype(BF16)
            strict, incl = ss < tt, ss <= tt
        else:
            tri = (tj >= ti).astype(BF16)
            strict, incl = ss > tt, ss >= tt
        lw = lw_ref[0]
        lw_hi = lw.astype(BF16)
        lw_mid = (lw - lw_hi.astype(F32)).astype(BF16)
        lw_lo = (lw - lw_hi.astype(F32) - lw_mid.astype(F32)).astype(BF16)
        cs = _dot(jnp.concatenate([tri, tri, tri], axis=1), jnp.concatenate([lw_hi, lw_mid, lw_lo], axis=0))
        gam = jnp.exp(cs)
        gam_inv = jnp.exp(-cs)
        kk = kk_ref[0]
        a_t = -(kk * jnp.exp(cs - lw))
        b_t = kk * e_ref[0] * gam_inv
        k_t = kd_ref[0] * gam_inv
        r_t = r_ref[0] * gam
        v_all = v_ref[0]
        g_end = gam[c - 1:c] if d == 0 else gam[0:1]
        for p in range(n_pair):
            sl = slice(2 * HEAD * p, 2 * HEAD * (p + 1))
            ch.append(dict(
                d=d, p=p, sl=sl, y_ref=y_ref, strict=strict, incl=incl, v2=stack(v_all[:, sl]),
                ar=jnp.concatenate([a_t[:, sl], r_t[:, sl]], axis=0).astype(BF16),
                bk=jnp.concatenate([stack(b_t[:, sl]), stack(k_t[:, sl])], axis=0),
                ge=g_end[:, sl], s0=s_ref[d, p]))
    for q in ch:
        q["m1"] = _dot_nt(q["ar"], q["bk"].astype(BF16))
    for q in ch:
        q["ars"] = _dot_nt(q["ar"], q["s0"].astype(BF16))
    for q in ch:
        m1 = q.pop("m1")
        q["pw"] = jnp.where(q["strict"], m1[:c, :2 * c], 0.0)
        lak = jnp.where(q["strict"], m1[:c, 2 * c:], 0.0).astype(BF16)
        q["lr"] = jnp.concatenate([jnp.where(q["incl"], m1[c:, :2 * c], 0.0),
                                   jnp.where(q["incl"], m1[c:, 2 * c:], 0.0)], axis=1).astype(BF16)
        q["x"] = q["ars"][:c] + _dot(lak, q["v2"].astype(BF16))
    for it in range(6):
        for q in ch:
            pb = q["pw"].astype(BF16)
            if it < 5:
                px = _dot(pb, jnp.concatenate([stack(q["x"]), stack(q["pw"])], axis=1).astype(BF16))
                q["x"] = q["x"] + px[:, :2 * c]
                q["pw"] = px[:, 2 * c:]
            else:
                q["x"] = q["x"] + _dot(pb, stack(q["x"]).astype(BF16))
    for q in ch:
        q["uv"] = jnp.concatenate([stack(q["x"]), q["v2"]], axis=0)
        q["y_ref"][0, :, q["sl"]] = q["ars"][c:] + _dot(q["lr"], q["uv"].astype(BF16))
    for q in ch:
        s_ref[q["d"], q["p"]] = (q["s0"] * q["ge"]
                                 + _dot(q["uv"].T.astype(BF16), (q["bk"] * q["ge"]).astype(BF16)))


def _rwkv_scan(r, v, kk, lw0, k0, e0, lw1, k1, e1, nctx_chunks):
    b, rows, d = r.shape
    c = CHUNK
    nc = rows // c
    fwd = lambda bb, i: (bb, i, 0)
    rev = lambda bb, i: (bb, jnp.where(i < nctx_chunks, nctx_chunks - 1 - i, nc - 1 - (i - nctx_chunks)), 0)
    sf = pl.BlockSpec((1, c, d), fwd)
    sr = pl.BlockSpec((1, c, d), rev)
    out = jax.ShapeDtypeStruct((b, rows, d), F32)
    return pl.pallas_call(
        _scan_kernel,
        grid=(b, nc),
        in_specs=[sf] * 6 + [sr] * 6,
        out_specs=[sf, sr],
        out_shape=[out, out],
        scratch_shapes=[pltpu.VMEM((2, d // (2 * HEAD), 2 * HEAD, 2 * HEAD), F32)],
        compiler_params=_cparams("parallel", "arbitrary"),
    )(r, v, kk, lw0, k0, e0, r, v, kk, lw1, k1, e1)


def _post_mixer(z, x_ref, mod_ref, stream, wo_ref, lnv, rw_ref, x1_ref, h2_ref, afft_ref):
    n, tm, d = x_ref.shape
    o = _dot(z.astype(BF16), wo_ref[...])
    x1s, h2s = [], []
    for s in range(n):
        m = mod_ref[2 * s + stream]
        x1 = _layer_norm(lnv[2:3] * x_ref[s] + m[2:3] * o[s * tm:(s + 1) * tm], lnv[0:1], lnv[1:2])
        x1s.append(x1)
        h2s.append(x1 * (1 + m[4:5]) + m[3:4])
    logits = jnp.dot(jnp.concatenate(h2s, axis=0), rw_ref[...], precision=HIGHEST, preferred_element_type=F32)
    lane = lax.broadcasted_iota(I32, logits.shape, 1)
    logits = jnp.where(lane < N_EXPERTS, logits, NEG)
    ex = jnp.exp(logits - jnp.max(logits, axis=-1, keepdims=True))
    aff = ex / jnp.sum(ex, axis=-1, keepdims=True)
    for s in range(n):
        aff_s = aff[s * tm:(s + 1) * tm]
        x1_ref[s] = x1s[s]
        h2_ref[s, :, :d] = h2s[s]
        h2_ref[s, :, d:] = aff_s
        afft_ref[s] = aff_s.T[:N_EXPERTS]


def _rwkv_out_kernel(nctx_blk, yf_ref, yr_ref, bonus_ref, g_ref, x_ref, mod_ref, lnx_ref, wo_ref, lnv_ref,
                     rw_ref, x1_ref, h2_ref, afft_ref):
    n, tm, d = x_ref.shape
    rows = lambda ref: ref[...].reshape(n * tm, d)
    y = rows(yf_ref) + rows(yr_ref)
    mu = _head_sum(y) * (1.0 / HEAD)
    yc = y - mu
    var = _head_sum(yc * yc) * (1.0 / HEAD)
    lnx = lnx_ref[...]
    yn = yc * lax.rsqrt(var + RW_LNX_EPS) * lnx[0:1] + lnx[1:2]
    z = (yn + rows(bonus_ref)) * rows(g_ref)
    stream = (pl.program_id(0) >= nctx_blk).astype(I32)
    _post_mixer(z, x_ref, mod_ref, stream, wo_ref, lnv_ref[...], rw_ref, x1_ref, h2_ref, afft_ref)


def _post_mixer_outs(b, rows, d, tm):
    specs = [pl.BlockSpec((b, tm, d), lambda i: (0, i, 0)),
             pl.BlockSpec((b, tm, d + LANES), lambda i: (0, i, 0)),
             pl.BlockSpec((b, N_EXPERTS, tm), lambda i: (0, 0, i))]
    shapes = [jax.ShapeDtypeStruct((b, rows, d), F32), jax.ShapeDtypeStruct((b, rows, d + LANES), F32),
              jax.ShapeDtypeStruct((b, N_EXPERTS, rows), F32)]
    return specs, shapes


def _rwkv_out(yf, yr, bonus, g, xa, mods, nctx_blk, lnx, wo, lnv, rw_pad):
    b, r, d = xa.shape
    tm = ROW_TILE
    row = pl.BlockSpec((b, tm, d), lambda i: (0, i, 0))
    full = lambda a: pl.BlockSpec(a.shape, lambda i: (0,) * a.ndim)
    out_specs, out_shape = _post_mixer_outs(b, r, d, tm)
    return pl.pallas_call(
        functools.partial(_rwkv_out_kernel, nctx_blk),
        grid=(r // tm,),
        in_specs=[row] * 5 + [full(mods), full(lnx), full(wo), full(lnv), full(rw_pad)],
        out_specs=out_specs,
        out_shape=out_shape,
        compiler_params=_cparams("arbitrary"),
    )(yf, yr, bonus, g, xa, mods, lnx, wo, lnv, rw_pad)


def _attn_out_kernel(o_ref, x_ref, mod_ref, wo_ref, lnv_ref, rw_ref, x1_ref, h2_ref, afft_ref):
    n, tm, d = x_ref.shape
    _post_mixer(o_ref[...].reshape(n * tm, d), x_ref, mod_ref, 1, wo_ref, lnv_ref[...], rw_ref,
                x1_ref, h2_ref, afft_ref)


def _attn_out(o, xa, mods, blk_off, wo, lnv, rw_pad):
    b, t, d = o.shape
    tm = ROW_TILE
    full = lambda a: pl.BlockSpec(a.shape, lambda i: (0,) * a.ndim)
    out_specs, out_shape = _post_mixer_outs(b, t, d, tm)
    return pl.pallas_call(
        _attn_out_kernel,
        grid=(t // tm,),
        in_specs=[pl.BlockSpec((b, tm, d), lambda i: (0, i, 0)),
                  pl.BlockSpec((b, tm, d), lambda i: (0, i + blk_off, 0)),
                  full(mods), full(wo), full(lnv), full(rw_pad)],
        out_specs=out_specs,
        out_shape=out_shape,
        compiler_params=_cparams("arbitrary"),
    )(o, xa, mods, wo, lnv, rw_pad)


def _select_kernel(cap, aff_ref, idx_ref, c_ref):
    n_e, tn = aff_ref.shape[1], aff_ref.shape[2]

    def bit_step(j, bits):
        cand = bits | jnp.left_shift(jnp.int32(1), 30 - j)
        cnt = jnp.sum((aff_ref[0] >= lax.bitcast_convert_type(cand, F32)).astype(F32), axis=1, keepdims=True)
        return jnp.where(cnt >= cap, cand, bits)

    bits = lax.fori_loop(0, 31, bit_step, jnp.zeros((n_e, 1), I32))
    thr = lax.bitcast_convert_type(bits, F32)
    need = cap - jnp.sum((aff_ref[0] > thr).astype(F32), axis=1, keepdims=True)
    triu = (lax.broadcasted_iota(I32, (LANES, LANES), 0) <= lax.broadcasted_iota(I32, (LANES, LANES), 1)).astype(BF16)
    tie_off = jnp.zeros((n_e, 1), F32)
    sel_off = jnp.zeros((n_e, 1), F32)
    for j in range(tn // LANES):
        cols = slice(j * LANES, (j + 1) * LANES)
        a = aff_ref[0, :, cols]
        eq = (a == thr).astype(F32)
        tie_rank = _dot(eq.astype(BF16), triu) + tie_off - eq
        sel = ((a > thr) | ((eq > 0) & (tie_rank < need))).astype(F32)
        c_ref[:, cols] = _dot(sel.astype(BF16), triu) + sel_off
        tie_off = tie_off + jnp.sum(eq, axis=1, keepdims=True)
        sel_off = sel_off + jnp.sum(sel, axis=1, keepdims=True)

    ns = min(cap, 256)
    tc = min(tn, 1024)
    lane = lax.broadcasted_iota(I32, (1, LANES), 1)
    for ci in range(cap // ns):
        slot = (lax.broadcasted_iota(I32, (ns, 1), 0) + ci * ns).astype(F32)

        def per_expert(e, out):
            def chunk(j, part):
                ce = c_ref[pl.ds(e, 1), pl.ds(pl.multiple_of(j * tc, tc), tc)]
                for k in range(tc // LANES):
                    part = part + (ce[:, k * LANES:(k + 1) * LANES] <= slot).astype(F32)
                return part

            part = lax.fori_loop(0, tn // tc, chunk, jnp.zeros((ns, LANES), F32))
            return jnp.where(lane == e, jnp.sum(part, axis=1, keepdims=True), out)

        out = lax.fori_loop(0, n_e, per_expert, jnp.zeros((ns, LANES), F32))
        idx_ref[0, ci * ns:(ci + 1) * ns, :] = out.astype(I32)


def _select(afft, cap):
    b, n_e, tn = afft.shape
    return pl.pallas_call(
        functools.partial(_select_kernel, cap),
        grid=(b,),
        in_specs=[pl.BlockSpec((1, n_e, tn), lambda bb: (bb, 0, 0))],
        out_specs=pl.BlockSpec((1, cap, LANES), lambda bb: (bb, 0, 0)),
        out_shape=jax.ShapeDtypeStruct((b, cap, LANES), I32),
        scratch_shapes=[pltpu.VMEM((n_e, tn), F32)],
        compiler_params=_cparams("parallel"),
    )(afft)


def _expert_kernel(idx_ref, h_hbm, wg_ref, wu_ref, wd_ref, acc_in, acc_hbm,
                   stage, accst, xin, gate, yacc, sem_x, sem_a, sem_s):
    del acc_in
    e = pl.program_id(0)
    f = pl.program_id(1)
    n_e = pl.num_programs(0)
    nf, mb, dx = stage.shape
    m, d = xin.shape
    ns, ng = EXPERT_SCATTER_STEPS, EXPERT_GATHER_STEPS
    bs, bg = nf // ns, nf // ng
    base = e * m
    prev = base - m
    nxt = lax.rem(e + 1, n_e) * m

    def x_copy(first, blk, u):
        return pltpu.make_async_copy(h_hbm.at[pl.ds(idx_ref[first + blk * mb + u], 1)],
                                     stage.at[blk, pl.ds(u, 1)], sem_x)

    def a_copy(blk, u):
        return pltpu.make_async_copy(acc_hbm.at[pl.ds(idx_ref[base + blk * mb + u], 1)],
                                     accst.at[blk, pl.ds(u, 1)], sem_a)

    def s_copy(first, blk, u):
        return pltpu.make_async_copy(accst.at[blk, pl.ds(u, 1)],
                                     acc_hbm.at[pl.ds(idx_ref[first + blk * mb + u], 1)], sem_s)

    def looped(copy_of):
        for blk in range(nf):
            def body(i, carry):
                for u in range(DMA_UNROLL):
                    copy_of(blk, i * DMA_UNROLL + u).start()
                return carry

            lax.fori_loop(0, mb // DMA_UNROLL, body, 0)

    def wait_blocks(src_of, dst_of, sem):
        for blk in range(nf):
            pltpu.make_async_copy(src_of(blk), dst_of(blk), sem).wait()

    wait_x = lambda: wait_blocks(lambda blk: h_hbm.at[pl.ds(0, mb)], lambda blk: stage.at[blk], sem_x)
    wait_s = lambda: wait_blocks(lambda blk: accst.at[blk], lambda blk: acc_hbm.at[pl.ds(0, mb)], sem_s)

    @pl.when((e == 0) & (f == 0))
    def _():
        looped(lambda blk, u: x_copy(base, blk, u))
        wait_x()

    @pl.when(f == 0)
    def _():
        rows = stage[...]
        xin[...] = rows[:, :, :d].reshape(m, d).astype(BF16)
        lane = lax.broadcasted_iota(I32, (1, 1, LANES), 2)
        gate[...] = jnp.sum(jnp.where(lane == e, rows[:, :, d:], 0.0), axis=2, keepdims=True).reshape(m, 1)
        yacc[...] = jnp.zeros_like(yacc)

    @pl.when((f == ns) & (e > 0))
    def _():
        wait_s()

    def step(extra_copies):
        for u in range(mb):
            x_copy(nxt, f, u).start()
        extra_copies()
        x = xin[...]
        a = _dot(x, wg_ref[0, 0].astype(BF16))
        up = _dot(x, wu_ref[0, 0].astype(BF16))
        hm = (a * jax.nn.sigmoid(a) * up).astype(BF16)
        wd = wd_ref[0, 0].astype(BF16)
        rb = m // 4
        for r0 in range(0, m, rb):
            yacc[r0:r0 + rb] += _dot(hm[r0:r0 + rb], wd)

    def scatter_copies():
        for k in range(bs):
            for u in range(mb):
                s_copy(prev, f * bs + k, u).start()

    def gather_copies():
        for k in range(bg):
            for u in range(mb):
                a_copy((f - ns) * bg + k, u).start()

    scatters = (f < ns) & (e > 0)
    gathers = (f >= ns) & (f < ns + ng)
    pl.when(scatters)(lambda: step(scatter_copies))
    pl.when(gathers)(lambda: step(gather_copies))
    pl.when(jnp.logical_not(scatters | gathers))(lambda: step(lambda: None))

    @pl.when(f == nf - 1)
    def _():
        wait_blocks(lambda blk: acc_hbm.at[pl.ds(0, mb)], lambda blk: accst.at[blk], sem_a)
        accst[...] = accst[...] + (gate[...] * yacc[...]).reshape(nf, mb, d)
        wait_x()

    @pl.when((f == nf - 1) & (e == n_e - 1))
    def _():
        looped(lambda blk, u: s_copy(base, blk, u))
        wait_s()


def _experts(idx_flat, m, h_rows, acc, layer, w_gate, w_up, w_down):
    _, n_e, d, f = w_gate.shape
    tf = EXPERT_F_TILE
    nf = f // tf
    mb = m // nf
    assert mb * nf == m and mb % DMA_UNROLL == 0 and m % 64 == 0 and h_rows.shape[1] == d + LANES
    assert nf % EXPERT_SCATTER_STEPS == 0 and nf % EXPERT_GATHER_STEPS == 0
    assert EXPERT_SCATTER_STEPS + EXPERT_GATHER_STEPS <= nf
    grid_spec = pltpu.PrefetchScalarGridSpec(
        num_scalar_prefetch=1,
        grid=(n_e, f // tf),
        in_specs=[pl.BlockSpec(memory_space=pl.ANY),
                  pl.BlockSpec((1, 1, d, tf), lambda e, j, idx: (layer, e, 0, j)),
                  pl.BlockSpec((1, 1, d, tf), lambda e, j, idx: (layer, e, 0, j)),
                  pl.BlockSpec((1, 1, tf, d), lambda e, j, idx: (layer, e, j, 0)),
                  pl.BlockSpec(memory_space=pl.ANY)],
        out_specs=pl.BlockSpec(memory_space=pl.ANY),
        scratch_shapes=[pltpu.VMEM((nf, mb, d + LANES), F32), pltpu.VMEM((nf, mb, d), F32),
                        pltpu.VMEM((m, d), BF16), pltpu.VMEM((m, 1), F32), pltpu.VMEM((m, d), F32),
                        pltpu.SemaphoreType.DMA(()), pltpu.SemaphoreType.DMA(()), pltpu.SemaphoreType.DMA(())],
    )
    return pl.pallas_call(
        _expert_kernel,
        grid_spec=grid_spec,
        out_shape=jax.ShapeDtypeStruct(acc.shape, F32),
        input_output_aliases={5: 0},
        compiler_params=_cparams("arbitrary", "arbitrary"),
    )(idx_flat, h_rows, w_gate, w_up, w_down, acc)


def _moe_ln_kernel(x_ref, moe_ref, mod_ref, lnv_ref, o_ref):
    lnv = lnv_ref[...]
    m = mod_ref[0]
    o_ref[0] = _layer_norm(lnv[2:3] * x_ref[0] + m[5:6] * moe_ref[0], lnv[0:1], lnv[1:2])


def _moe_ln(x1, moe, mods, nctx_blk, lnv):
    b, r, d = x1.shape
    tm = ROW_TILE
    row = pl.BlockSpec((1, tm, d), lambda bb, i: (bb, i, 0))
    return pl.pallas_call(
        _moe_ln_kernel,
        grid=(b, r // tm),
        in_specs=[row, row,
                  pl.BlockSpec((1, N_MOD, d), lambda bb, i: (bb * 2 + (i >= nctx_blk).astype(I32), 0, 0)),
                  pl.BlockSpec(lnv.shape, lambda bb, i: (0, 0))],
        out_specs=row,
        out_shape=jax.ShapeDtypeStruct((b, r, d), F32),
        compiler_params=_cparams("parallel", "arbitrary"),
    )(x1, moe, mods, lnv)


def _ec_moe(h2x, afft, streams, layer, w_gate, w_up, w_down):
    b, r, dx = h2x.shape
    d = dx - LANES
    idx_parts = []
    for off, tn in streams:
        cap = EC_CAPACITY * tn // N_EXPERTS
        idx = _select(lax.slice_in_dim(afft, off, off + tn, axis=2), cap)[:, :, :N_EXPERTS]
        rows = idx + (jnp.arange(b, dtype=I32) * r + off)[:, None, None]
        idx_parts.append(jnp.transpose(rows, (2, 0, 1)).reshape(N_EXPERTS, b * cap))
    idx_flat = jnp.concatenate(idx_parts, axis=1)
    acc = _experts(idx_flat.reshape(-1), idx_flat.shape[1], h2x.reshape(b * r, dx),
                   jnp.zeros((b * r, d), F32), layer, w_gate, w_up, w_down)
    return acc.reshape(b, r, d)


def _qkv_kernel(x_ref, mod_ref, cos_ref, sin_ref, wq_ref, wk_ref, wv_ref, q_ref, k_ref, v_ref):
    m = mod_ref[0]
    h = (x_ref[0] * (1 + m[1:2]) + m[0:1]).astype(BF16)
    d = x_ref.shape[2]
    rep = d // LANES
    cos = jnp.concatenate([cos_ref[...]] * rep, axis=1)
    sin = jnp.concatenate([sin_ref[...]] * rep, axis=1)
    lane = lax.broadcasted_iota(I32, (1, d), 1)
    lower = (lane % 32) < 16

    def rope(t):
        partner = jnp.where(lower, pltpu.roll(t, d - 16, 1), pltpu.roll(t, 16, 1))
        return t * cos + partner * sin

    q_ref[0] = (rope(_dot(h, wq_ref[...])) * (HEAD ** -0.5)).astype(BF16)
    k_ref[0] = rope(_dot(h, wk_ref[...])).astype(BF16)
    v_ref[0] = _dot(h, wv_ref[...]).astype(BF16)


def _qkv(xa, mods, nctx_blk, cos, sin, wq, wk4, wv4):
    b, r, d = xa.shape
    tm = ROW_TILE
    row = pl.BlockSpec((1, tm, d), lambda bb, i: (bb, i, 0))
    tab = pl.BlockSpec((tm, LANES), lambda bb, i: (i, 0))
    full = lambda a: pl.BlockSpec(a.shape, lambda bb, i: (0,) * a.ndim)
    out = jax.ShapeDtypeStruct((b, r, d), BF16)
    return pl.pallas_call(
        _qkv_kernel,
        grid=(b, r // tm),
        in_specs=[row, pl.BlockSpec((1, N_MOD, d), lambda bb, i: (bb * 2 + (i >= nctx_blk).astype(I32), 0, 0)),
                  tab, tab, full(wq), full(wk4), full(wv4)],
        out_specs=[row] * 3,
        out_shape=[out] * 3,
        compiler_params=_cparams("parallel", "arbitrary"),
    )(xa, mods, cos, sin, wq, wk4, wv4)


def _attn_kernel(n_ctx, sink_ref, q_ref, kc_ref, vc_ref, kp_ref, k0_ref, kn_ref, vp_ref, v0_ref, vn_ref, o_ref):
    nb = pl.program_id(1)
    n_blk = pl.num_programs(1)
    bq = q_ref.shape[1]
    gw = ATT_GROUP * HEAD
    nk = n_ctx + 3 * bq
    col = lax.broadcasted_iota(I32, (bq, nk), 1)
    qpos = nb * bq + lax.broadcasted_iota(I32, (bq, nk), 0)
    kpos = (nb - 1) * bq + (col - n_ctx)
    valid = (col < n_ctx) | ((kpos >= 0) & (kpos < n_blk * bq) & (jnp.abs(qpos - kpos) <= WINDOW))
    lane = lax.broadcasted_iota(I32, (1, gw), 1)
    heads = []
    for h in range(ATT_KV_HEADS):
        sl = slice(gw * h, gw * (h + 1))
        keys = jnp.concatenate([kc_ref[0, :, sl], kp_ref[0, :, sl], k0_ref[0, :, sl], kn_ref[0, :, sl]], axis=0)
        vals = jnp.concatenate([vc_ref[0, :, sl], vp_ref[0, :, sl], v0_ref[0, :, sl], vn_ref[0, :, sl]], axis=0)
        qh = q_ref[0, :, sl]
        scores = [_dot_nt(jnp.where((lane // HEAD) == g, qh, jnp.zeros_like(qh)), keys)
                  for g in range(ATT_GROUP)]
        heads.append((sl, vals, scores))
    probs = []
    for h, (sl, vals, scores) in enumerate(heads):
        row = []
        for g, s in enumerate(scores):
            s = jnp.where(valid, s, NEG)
            sink = sink_ref[h * ATT_GROUP + g]
            mx = jnp.maximum(jnp.max(s, axis=-1, keepdims=True), sink)
            p = jnp.exp(s - mx)
            den = jnp.sum(p, axis=-1, keepdims=True) + jnp.exp(sink - mx)
            row.append((p / den).astype(BF16))
        probs.append(row)
    for (sl, vals, _), row in zip(heads, probs):
        acc = jnp.zeros((bq, gw), F32)
        for g, p in enumerate(row):
            acc = acc + jnp.where((lane // HEAD) == g, _dot(p, vals), 0.0)
        o_ref[0, :, sl] = acc.astype(BF16)


def _attention(q, k4, v4, sink, n_ctx):
    b, r, d = q.shape
    bq = WINDOW
    t = r - n_ctx
    nb = t // bq
    off = n_ctx // bq
    blk = lambda f: pl.BlockSpec((1, bq, d), f)
    ctx_spec = pl.BlockSpec((1, n_ctx, d), lambda bb, i: (bb, 0, 0))
    prev = lambda bb, i: (bb, jnp.maximum(i - 1, 0) + off, 0)
    cur = lambda bb, i: (bb, i + off, 0)
    nxt = lambda bb, i: (bb, jnp.minimum(i + 1, nb - 1) + off, 0)
    return pl.pallas_call(
        functools.partial(_attn_kernel, n_ctx),
        grid=(b, nb),
        in_specs=[pl.BlockSpec(memory_space=pltpu.SMEM), blk(cur), ctx_spec, ctx_spec,
                  blk(prev), blk(cur), blk(nxt), blk(prev), blk(cur), blk(nxt)],
        out_specs=pl.BlockSpec((1, bq, d), lambda bb, i: (bb, i, 0)),
        out_shape=jax.ShapeDtypeStruct((b, t, d), BF16),
        compiler_params=_cparams("parallel", "arbitrary"),
    )(sink, q, k4, v4, k4, k4, k4, v4, v4, v4)


def _rope_tables(n_ctx, t):
    half = HEAD // 2
    inv = 1.0 / (ROPE_BASE ** (jnp.arange(0, half, 2, dtype=F32) / half))
    pos = jnp.arange(t)
    ang_r = (pos // GRID_W).astype(F32)[:, None] * inv
    ang_c = (pos % GRID_W).astype(F32)[:, None] * inv
    cos = jnp.concatenate([jnp.cos(ang_r)] * 2 + [jnp.cos(ang_c)] * 2, axis=1)
    sin = jnp.concatenate([-jnp.sin(ang_r), jnp.sin(ang_r), -jnp.sin(ang_c), jnp.sin(ang_c)], axis=1)
    cos = jnp.concatenate([jnp.ones((n_ctx, HEAD), F32), cos], axis=0)
    sin = jnp.concatenate([jnp.zeros((n_ctx, HEAD), F32), sin], axis=0)
    return jnp.tile(cos, (1, 2)), jnp.tile(sin, (1, 2))


def kernel(x, c, ctx, c_ctx, ada_w, ada_b, ln1_g, ln1_b, ln2_g, ln2_b, rw_mu, rw_wr, rw_wk, rw_wv, rw_w0, rw_w1, rw_w2, rw_a0, rw_a1, rw_a2, rw_g1, rw_g2, rw_kk, rw_ka, rw_rk, rw_lnx_g, rw_lnx_b, rw_wo, at_wqkv, at_wo, at_sink, router_w, moe_w_gate, moe_w_up, moe_w_down):
    b, t, d = x.shape
    n_ctx = ctx.shape[1]
    depth = ada_w.shape[0]
    alpha = (2 * depth) ** 0.25
    assert n_ctx % ROW_TILE == 0 and t % ROW_TILE == 0 and b + 1 <= 8
    nctx_blk = n_ctx // ROW_TILE

    cc = jnp.concatenate([c, c_ctx[None], jnp.zeros((8 - b - 1, d), F32)], axis=0)
    mod_all = _ada_mod(cc, ada_w, ada_b).reshape(depth, 8, N_MOD, d)
    xa = jnp.concatenate([ctx, x], axis=1)
    bf = lambda w: w.astype(BF16)
    pad_rows = lambda w, lo: jnp.pad(w, ((lo, LANES - lo - w.shape[0]), (0, 0)))

    for i in range(depth):
        last = i == depth - 1
        j = i // 2
        mods = jnp.stack([jnp.broadcast_to(mod_all[i, b], (b, N_MOD, d)), mod_all[i, :b]], axis=1).reshape(2 * b, N_MOD, d)
        lnv1 = jnp.stack([ln1_g[i], ln1_b[i], jnp.full((d,), alpha, F32)])
        lnv2 = jnp.stack([ln2_g[i], ln2_b[i], jnp.full((d,), alpha, F32)])
        rw_pad = jnp.pad(router_w[i], ((0, 0), (0, LANES - N_EXPERTS)))
        if i % 2 == 0:
            vec = jnp.concatenate([rw_mu[j], rw_w0[j], rw_a0[j], rw_kk[j][None], rw_ka[j][None],
                                   rw_rk[j].reshape(1, d), jnp.zeros((3, d), F32)], axis=0)
            w1 = bf(jnp.concatenate([rw_w1[j, 0], rw_w1[j, 1]], axis=1))
            a1 = bf(jnp.concatenate([rw_a1[j, 0], rw_a1[j, 1]], axis=1))
            w2 = bf(jnp.stack([pad_rows(rw_w2[j, 0], 0), pad_rows(rw_w2[j, 1], HEAD)]))
            a2 = bf(jnp.stack([pad_rows(rw_a2[j, 0], 0), pad_rows(rw_a2[j, 1], HEAD)]))
            (r, v, kk, g, bonus, lw0, k0, e0, lw1, k1, e1) = _rwkv_proj(
                xa, mods, nctx_blk, vec, bf(rw_wr[j]), bf(rw_wk[j]), bf(rw_wv[j]), w1, a1, bf(rw_g1[j]),
                w2, a2, bf(rw_g2[j]))
            yf, yr = _rwkv_scan(r, v, kk, lw0, k0, e0, lw1, k1, e1, n_ctx // CHUNK)
            lnx = jnp.stack([rw_lnx_g[j], rw_lnx_b[j]])
            x1, h2, afft = _rwkv_out(yf, yr, bonus, g, xa, mods, nctx_blk, lnx, bf(rw_wo[j]), lnv1, rw_pad)
        else:
            nq = ATT_KV_HEADS * ATT_GROUP * HEAD
            nk = ATT_KV_HEADS * HEAD
            wqkv = at_wqkv[j]
            tile4 = lambda w: jnp.tile(w.reshape(d, ATT_KV_HEADS, 1, HEAD), (1, 1, ATT_GROUP, 1)).reshape(d, nq)
            cos, sin = _rope_tables(n_ctx, t)
            q, k4, v4 = _qkv(xa, mods, nctx_blk, cos, sin, bf(wqkv[:, :nq]),
                             bf(tile4(wqkv[:, nq:nq + nk])), bf(tile4(wqkv[:, nq + nk:])))
            o = _attention(q, k4, v4, at_sink[j], n_ctx)
            x1, h2, afft = _attn_out(o, xa, mods, nctx_blk, bf(at_wo[j]), lnv1, rw_pad)
        if last:
            if x1.shape[1] != t:
                x1, h2, afft = x1[:, n_ctx:], h2[:, n_ctx:], afft[:, :, n_ctx:]
            moe = _ec_moe(h2, afft, [(0, t)], i, moe_w_gate, moe_w_up, moe_w_down)
            return _moe_ln(x1, moe, mods, 0, lnv2)
        moe = _ec_moe(h2, afft, [(n_ctx, t), (0, n_ctx)], i, moe_w_gate, moe_w_up, moe_w_down)
        xa = _moe_ln(x1, moe, mods, nctx_blk, lnv2)
    return xa[:, n_ctx:]
```

```python
import functools

import jax
import jax.numpy as jnp
from jax import lax
from jax.experimental import pallas as pl
from jax.experimental.pallas import tpu as pltpu

F32 = jnp.float32
BF16 = jnp.bfloat16
I32 = jnp.int32
HIGHEST = lax.Precision.HIGHEST

LANES = 128
LN_EPS = 1e-5
N_MOD = 6
HEAD = 64
RW_LNX_EPS = 64e-5
RW_DECAY_SCALE = 0.606531
ATT_KV_HEADS = 4
ATT_GROUP = 4
WINDOW = 128
GRID_W = 64
ROPE_BASE = 10000.0
NEG = -1e30
N_EXPERTS = 16
EC_CAPACITY = 2
CHUNK = 64
DMA_UNROLL = 8
EXPERT_F_TILE = 256
EXPERT_SCATTER_STEPS = 2
EXPERT_GATHER_STEPS = 4
ROW_TILE = 256
VMEM_LIMIT = 56 * 2 ** 20


def _cparams(*sem):
    return pltpu.CompilerParams(dimension_semantics=sem, vmem_limit_bytes=VMEM_LIMIT)


def _dot(a, b):
    return jnp.dot(a, b, preferred_element_type=F32)


def _dot_nt(a, b):
    return lax.dot_general(a, b, (((1,), (1,)), ((), ())), preferred_element_type=F32)


def _split(x):
    hi = x.astype(BF16)
    lo = (x - hi.astype(F32)).astype(BF16)
    return hi, lo


def _head_sum(x):
    d = x.shape[-1]
    gr = (lax.broadcasted_iota(I32, (d, LANES), 0) // HEAD
          == lax.broadcasted_iota(I32, (d, LANES), 1)).astype(BF16)
    ge = (lax.broadcasted_iota(I32, (LANES, d), 0)
          == lax.broadcasted_iota(I32, (LANES, d), 1) // HEAD).astype(BF16)
    s = _dot(jnp.concatenate(_split(x), axis=1), jnp.concatenate([gr, gr], axis=0))
    return _dot(jnp.concatenate(_split(s), axis=1), jnp.concatenate([ge, ge], axis=0))


def _layer_norm(x, g, b):
    mu = jnp.mean(x, axis=-1, keepdims=True)
    xc = x - mu
    var = jnp.mean(xc * xc, axis=-1, keepdims=True)
    return xc * lax.rsqrt(var + LN_EPS) * g + b


def _ada_kernel(cc_ref, w_ref, b_ref, o_ref):
    cc = cc_ref[...]
    s = cc * jax.nn.sigmoid(cc)
    o_ref[0] = jnp.dot(s, w_ref[0], precision=HIGHEST, preferred_element_type=F32) + b_ref[0]


def _ada_mod(cc, ada_w, ada_b):
    depth, d, n = ada_w.shape
    tn = n // 4
    return pl.pallas_call(
        _ada_kernel,
        grid=(depth, n // tn),
        in_specs=[pl.BlockSpec((8, d), lambda l, j: (0, 0)),
                  pl.BlockSpec((1, d, tn), lambda l, j: (l, 0, j)),
                  pl.BlockSpec((1, 1, tn), lambda l, j: (l, 0, j))],
        out_specs=pl.BlockSpec((1, 8, tn), lambda l, j: (l, 0, j)),
        out_shape=jax.ShapeDtypeStruct((depth, 8, n), F32),
        compiler_params=_cparams("arbitrary", "arbitrary"),
    )(cc, ada_w, ada_b.reshape(depth, 1, n))


def _rwkv_proj_kernel(nctx_blk, x_ref, xp_ref, xn_ref, mod_ref, vec_ref, wr_ref, wk_ref, wv_ref,
                      w1_ref, a1_ref, g1_ref, w2_ref, a2_ref, g2_ref,
                      r_ref, v_ref, kk_ref, g_ref, bonus_ref,
                      lw0_ref, k0_ref, e0_ref, lw1_ref, k1_ref, e1_ref):
    i = pl.program_id(1)
    nblk = pl.num_programs(1)
    tm = x_ref.shape[1]
    m = mod_ref[0]
    shift, scale = m[0:1], m[1:2]
    h = x_ref[0] * (1 + scale) + shift
    first = (i == 0) | (i == nctx_blk)
    last = (i == nctx_blk - 1) | (i == nblk - 1)
    hp = jnp.where(first, 0.0, xp_ref[0][7:8] * (1 + scale) + shift)
    hn = jnp.where(last, 0.0, xn_ref[0][0:1] * (1 + scale) + shift)
    rows = lax.broadcasted_iota(I32, (tm, 1), 0)
    prev = jnp.where(rows == 0, hp, pltpu.roll(h, 1, 0))
    nxt = jnp.where(rows == tm - 1, hn, pltpu.roll(h, tm - 1, 0))
    xx = 0.5 * (prev + nxt) - h
    vec = vec_ref[...]
    mix = lambda j: (h + xx * vec[j:j + 1]).astype(BF16)
    r = _dot(mix(0), wr_ref[...])
    k = _dot(mix(2), wk_ref[...])
    v = _dot(mix(3), wv_ref[...])
    tw = jnp.tanh(_dot(mix(1), w1_ref[...])).astype(BF16)
    ta = _dot(mix(4), a1_ref[...]).astype(BF16)
    g = _dot(jax.nn.sigmoid(_dot(mix(5), g1_ref[...])).astype(BF16), g2_ref[...])
    kk = k * vec[10:11]
    nrm = jnp.sqrt(_head_sum(kk * kk))
    kk = kk / jnp.maximum(nrm, 1e-12)
    bonus = _head_sum(r * k * vec[12:13]) * v
    r_ref[0] = r
    v_ref[0] = v
    kk_ref[0] = kk
    g_ref[0] = g
    bonus_ref[0] = bonus
    for d, (lw_ref, kd_ref, e_ref) in enumerate(((lw0_ref, k0_ref, e0_ref), (lw1_ref, k1_ref, e1_ref))):
        lw_ref[0] = -RW_DECAY_SCALE * jax.nn.sigmoid(vec[6 + d:7 + d] + _dot(tw, w2_ref[d]))
        eta = jax.nn.sigmoid(vec[8 + d:9 + d] + _dot(ta, a2_ref[d]))
        e_ref[0] = eta
        kd_ref[0] = k * (1 + (eta - 1) * vec[11:12])


def _rwkv_proj(xa, mods, nctx_blk, vec, wr, wk, wv, w1, a1, g1, w2, a2, g2):
    b, r, d = xa.shape
    tm = ROW_TILE
    nblk = r // tm
    hb = tm // 8
    row = pl.BlockSpec((1, tm, d), lambda bb, i: (bb, i, 0))
    full = lambda a: pl.BlockSpec(a.shape, lambda bb, i: (0,) * a.ndim)
    out = jax.ShapeDtypeStruct((b, r, d), F32)
    return pl.pallas_call(
        functools.partial(_rwkv_proj_kernel, nctx_blk),
        grid=(b, nblk),
        in_specs=[row,
                  pl.BlockSpec((1, 8, d), lambda bb, i: (bb, jnp.maximum(i * hb - 1, 0), 0)),
                  pl.BlockSpec((1, 8, d), lambda bb, i: (bb, jnp.minimum((i + 1) * hb, r // 8 - 1), 0)),
                  pl.BlockSpec((1, N_MOD, d), lambda bb, i: (bb * 2 + (i >= nctx_blk).astype(I32), 0, 0)),
                  full(vec), full(wr), full(wk), full(wv), full(w1), full(a1), full(g1),
                  full(w2), full(a2), full(g2)],
        out_specs=[row] * 11,
        out_shape=[out] * 11,
        compiler_params=_cparams("parallel", "arbitrary"),
    )(xa, xa, xa, mods, vec, wr, wk, wv, w1, a1, g1, w2, a2, g2)


def _scan_kernel(rf_ref, vf_ref, kkf_ref, lw0_ref, k0_ref, e0_ref,
                 rr_ref, vr_ref, kkr_ref, lw1_ref, k1_ref, e1_ref,
                 yf_ref, yr_ref, s_ref):
    c = CHUNK
    assert c == HEAD

    @pl.when(pl.program_id(1) == 0)
    def _():
        s_ref[...] = jnp.zeros_like(s_ref)

    ti = lax.broadcasted_iota(I32, (c, c), 0)
    tj = lax.broadcasted_iota(I32, (c, c), 1)
    tt = lax.broadcasted_iota(I32, (c, 2 * c), 0)
    ss = lax.broadcasted_iota(I32, (c, 2 * c), 1) % c
    m0 = lax.broadcasted_iota(I32, (1, 2 * HEAD), 1) < HEAD

    def stack(x):
        return jnp.concatenate([jnp.where(m0, x, 0.0), jnp.where(m0, 0.0, x)], axis=0)

    dirs = ((rf_ref, vf_ref, kkf_ref, lw0_ref, k0_ref, e0_ref, yf_ref),
            (rr_ref, vr_ref, kkr_ref, lw1_ref, k1_ref, e1_ref, yr_ref))
    n_pair = s_ref.shape[1]
    ch = []
    for d, (r_ref, v_ref, kk_ref, lw_ref, kd_ref, e_ref, y_ref) in enumerate(dirs):
        if d == 0:
            tri = (tj <= ti).astype(BF16)
            strict, incl = ss < tt, ss <= tt
        else:
            tri = (tj >= ti).astype(BF16)
            strict, incl = ss > tt, ss >= tt
        lw = lw_ref[0]
        lw_hi = lw.astype(BF16)
        lw_mid = (lw - lw_hi.astype(F32)).astype(BF16)
        lw_lo = (lw - lw_hi.astype(F32) - lw_mid.astype(F32)).astype(BF16)
        cs = _dot(jnp.concatenate([tri, tri, tri], axis=1), jnp.concatenate([lw_hi, lw_mid, lw_lo], axis=0))
        gam = jnp.exp(cs)
        gam_inv = jnp.exp(-cs)
        kk = kk_ref[0]
        a_t = -(kk * jnp.exp(cs - lw))
        b_t = kk * e_ref[0] * gam_inv
        k_t = kd_ref[0] * gam_inv
        r_t = r_ref[0] * gam
        v_all = v_ref[0]
        g_end = gam[c - 1:c] if d == 0 else gam[0:1]
        for p in range(n_pair):
            sl = slice(2 * HEAD * p, 2 * HEAD * (p + 1))
            ch.append(dict(
                d=d, p=p, sl=sl, y_ref=y_ref, strict=strict, incl=incl, v2=stack(v_all[:, sl]),
                ar=jnp.concatenate([a_t[:, sl], r_t[:, sl]], axis=0).astype(BF16),
                bk=jnp.concatenate([stack(b_t[:, sl]), stack(k_t[:, sl])], axis=0),
                ge=g_end[:, sl], s0=s_ref[d, p]))
    for q in ch:
        q["m1"] = _dot_nt(q["ar"], q["bk"].astype(BF16))
    for q in ch:
        q["ars"] = _dot_nt(q["ar"], q["s0"].astype(BF16))
    for q in ch:
        m1 = q.pop("m1")
        q["pw"] = jnp.where(q["strict"], m1[:c, :2 * c], 0.0)
        lak = jnp.where(q["strict"], m1[:c, 2 * c:], 0.0).astype(BF16)
        q["lr"] = jnp.concatenate([jnp.where(q["incl"], m1[c:, :2 * c], 0.0),
                                   jnp.where(q["incl"], m1[c:, 2 * c:], 0.0)], axis=1).astype(BF16)
        q["x"] = q["ars"][:c] + _dot(lak, q["v2"].astype(BF16))
    for it in range(6):
        for q in ch:
            pb = q["pw"].astype(BF16)
            if it < 5:
                px = _dot(pb, jnp.concatenate([stack(q["x"]), stack(q["pw"])], axis=1).astype(BF16))
                q["x"] = q["x"] + px[:, :2 * c]
                q["pw"] = px[:, 2 * c:]
            else:
                q["x"] = q["x"] + _dot(pb, stack(q["x"]).astype(BF16))
    for q in ch:
        q["uv"] = jnp.concatenate([stack(q["x"]), q["v2"]], axis=0)
        q["y_ref"][0, :, q["sl"]] = q["ars"][c:] + _dot(q["lr"], q["uv"].astype(BF16))
    for q in ch:
        s_ref[q["d"], q["p"]] = (q["s0"] * q["ge"]
                                 + _dot(q["uv"].T.astype(BF16), (q["bk"] * q["ge"]).astype(BF16)))


def _rwkv_scan(r, v, kk, lw0, k0, e0, lw1, k1, e1, nctx_chunks):
    b, rows, d = r.shape
    c = CHUNK
    nc = rows // c
    fwd = lambda bb, i: (bb, i, 0)
    rev = lambda bb, i: (bb, jnp.where(i < nctx_chunks, nctx_chunks - 1 - i, nc - 1 - (i - nctx_chunks)), 0)
    sf = pl.BlockSpec((1, c, d), fwd)
    sr = pl.BlockSpec((1, c, d), rev)
    out = jax.ShapeDtypeStruct((b, rows, d), F32)
    return pl.pallas_call(
        _scan_kernel,
        grid=(b, nc),
        in_specs=[sf] * 6 + [sr] * 6,
        out_specs=[sf, sr],
        out_shape=[out, out],
        scratch_shapes=[pltpu.VMEM((2, d // (2 * HEAD), 2 * HEAD, 2 * HEAD), F32)],
        compiler_params=_cparams("parallel", "arbitrary"),
    )(r, v, kk, lw0, k0, e0, r, v, kk, lw1, k1, e1)


def _post_mixer(z, x_ref, mod_ref, stream, wo_ref, lnv, rw_ref, x1_ref, h2_ref, afft_ref):
    n, tm, d = x_ref.shape
    o = _dot(z.astype(BF16), wo_ref[...])
    x1s, h2s = [], []
    for s in range(n):
        m = mod_ref[2 * s + stream]
        x1 = _layer_norm(lnv[2:3] * x_ref[s] + m[2:3] * o[s * tm:(s + 1) * tm], lnv[0:1], lnv[1:2])
        x1s.append(x1)
        h2s.append(x1 * (1 + m[4:5]) + m[3:4])
    logits = jnp.dot(jnp.concatenate(h2s, axis=0), rw_ref[...], precision=HIGHEST, preferred_element_type=F32)
    lane = lax.broadcasted_iota(I32, logits.shape, 1)
    logits = jnp.where(lane < N_EXPERTS, logits, NEG)
    ex = jnp.exp(logits - jnp.max(logits, axis=-1, keepdims=True))
    aff = ex / jnp.sum(ex, axis=-1, keepdims=True)
    for s in range(n):
        aff_s = aff[s * tm:(s + 1) * tm]
        x1_ref[s] = x1s[s]
        h2_ref[s, :, :d] = h2s[s]
        h2_ref[s, :, d:] = aff_s
        afft_ref[s] = aff_s.T[:N_EXPERTS]


def _rwkv_out_kernel(nctx_blk, yf_ref, yr_ref, bonus_ref, g_ref, x_ref, mod_ref, lnx_ref, wo_ref, lnv_ref,
                     rw_ref, x1_ref, h2_ref, afft_ref):
    n, tm, d = x_ref.shape
    rows = lambda ref: ref[...].reshape(n * tm, d)
    y = rows(yf_ref) + rows(yr_ref)
    mu = _head_sum(y) * (1.0 / HEAD)
    yc = y - mu
    var = _head_sum(yc * yc) * (1.0 / HEAD)
    lnx = lnx_ref[...]
    yn = yc * lax.rsqrt(var + RW_LNX_EPS) * lnx[0:1] + lnx[1:2]
    z = (yn + rows(bonus_ref)) * rows(g_ref)
    stream = (pl.program_id(0) >= nctx_blk).astype(I32)
    _post_mixer(z, x_ref, mod_ref, stream, wo_ref, lnv_ref[...], rw_ref, x1_ref, h2_ref, afft_ref)


def _post_mixer_outs(b, rows, d, tm):
    specs = [pl.BlockSpec((b, tm, d), lambda i: (0, i, 0)),
             pl.BlockSpec((b, tm, d + LANES), lambda i: (0, i, 0)),
             pl.BlockSpec((b, N_EXPERTS, tm), lambda i: (0, 0, i))]
    shapes = [jax.ShapeDtypeStruct((b, rows, d), F32), jax.ShapeDtypeStruct((b, rows, d + LANES), F32),
              jax.ShapeDtypeStruct((b, N_EXPERTS, rows), F32)]
    return specs, shapes


def _rwkv_out(yf, yr, bonus, g, xa, mods, nctx_blk, lnx, wo, lnv, rw_pad):
    b, r, d = xa.shape
    tm = ROW_TILE
    row = pl.BlockSpec((b, tm, d), lambda i: (0, i, 0))
    full = lambda a: pl.BlockSpec(a.shape, lambda i: (0,) * a.ndim)
    out_specs, out_shape = _post_mixer_outs(b, r, d, tm)
    return pl.pallas_call(
        functools.partial(_rwkv_out_kernel, nctx_blk),
        grid=(r // tm,),
        in_specs=[row] * 5 + [full(mods), full(lnx), full(wo), full(lnv), full(rw_pad)],
        out_specs=out_specs,
        out_shape=out_shape,
        compiler_params=_cparams("arbitrary"),
    )(yf, yr, bonus, g, xa, mods, lnx, wo, lnv, rw_pad)


def _attn_out_kernel(o_ref, x_ref, mod_ref, wo_ref, lnv_ref, rw_ref, x1_ref, h2_ref, afft_ref):
    n, tm, d = x_ref.shape
    _post_mixer(o_ref[...].reshape(n * tm, d), x_ref, mod_ref, 1, wo_ref, lnv_ref[...], rw_ref,
                x1_ref, h2_ref, afft_ref)


def _attn_out(o, xa, mods, blk_off, wo, lnv, rw_pad):
    b, t, d = o.shape
    tm = ROW_TILE
    full = lambda a: pl.BlockSpec(a.shape, lambda i: (0,) * a.ndim)
    out_specs, out_shape = _post_mixer_outs(b, t, d, tm)
    return pl.pallas_call(
        _attn_out_kernel,
        grid=(t // tm,),
        in_specs=[pl.BlockSpec((b, tm, d), lambda i: (0, i, 0)),
                  pl.BlockSpec((b, tm, d), lambda i: (0, i + blk_off, 0)),
                  full(mods), full(wo), full(lnv), full(rw_pad)],
        out_specs=out_specs,
        out_shape=out_shape,
        compiler_params=_cparams("arbitrary"),
    )(o, xa, mods, wo, lnv, rw_pad)


def _select_kernel(cap, aff_ref, idx_ref, c_ref, bnd_v, bnd_s, sem):
    n_e, tn = aff_ref.shape[1], aff_ref.shape[2]

    def bit_step(j, bits):
        cand = bits | jnp.left_shift(jnp.int32(1), 30 - j)
        cnt = jnp.sum((aff_ref[0] >= lax.bitcast_convert_type(cand, F32)).astype(F32), axis=1, keepdims=True)
        return jnp.where(cnt >= cap, cand, bits)

    bits = lax.fori_loop(0, 31, bit_step, jnp.zeros((n_e, 1), I32))
    thr = lax.bitcast_convert_type(bits, F32)
    need = cap - jnp.sum((aff_ref[0] > thr).astype(F32), axis=1, keepdims=True)
    triu = (lax.broadcasted_iota(I32, (LANES, LANES), 0) <= lax.broadcasted_iota(I32, (LANES, LANES), 1)).astype(BF16)
    ns = min(cap, 256)
    tc = min(tn, 1024)
    lane = lax.broadcasted_iota(I32, (1, LANES), 1)
    tie_off = jnp.zeros((n_e, 1), F32)
    sel_off = jnp.zeros((n_e, 1), F32)
    bnd = jnp.zeros((n_e, LANES), F32)
    for j in range(tn // LANES):
        cols = slice(j * LANES, (j + 1) * LANES)
        a = aff_ref[0, :, cols]
        eq = (a == thr).astype(F32)
        tie_rank = _dot(eq.astype(BF16), triu) + tie_off - eq
        sel = ((a > thr) | ((eq > 0) & (tie_rank < need))).astype(F32)
        c_ref[:, cols] = _dot(sel.astype(BF16), triu) + sel_off
        tie_off = tie_off + jnp.sum(eq, axis=1, keepdims=True)
        sel_off = sel_off + jnp.sum(sel, axis=1, keepdims=True)
        if (j + 1) % (tc // LANES) == 0:
            bnd = jnp.where(lane == (j + 1) // (tc // LANES), sel_off, bnd)
    bnd_v[...] = bnd.astype(I32)
    to_smem = pltpu.make_async_copy(bnd_v, bnd_s, sem)
    to_smem.start()
    to_smem.wait()

    for ci in range(cap // ns):
        slot = (lax.broadcasted_iota(I32, (ns, 1), 0) + ci * ns).astype(F32)

        def per_expert(e, out):
            def chunk(j, part):
                def compare():
                    ce = c_ref[pl.ds(e, 1), pl.ds(pl.multiple_of(j * tc, tc), tc)]
                    acc = part
                    for k in range(tc // LANES):
                        acc = acc + (ce[:, k * LANES:(k + 1) * LANES] <= slot).astype(F32)
                    return acc

                before, upto = bnd_s[e, j], bnd_s[e, j + 1]
                return lax.cond(upto <= ci * ns, lambda: part + float(tc // LANES),
                                lambda: lax.cond(before > ci * ns + ns - 1, lambda: part, compare))

            part = lax.fori_loop(0, tn // tc, chunk, jnp.zeros((ns, LANES), F32))
            return jnp.where(lane == e, jnp.sum(part, axis=1, keepdims=True), out)

        out = lax.fori_loop(0, n_e, per_expert, jnp.zeros((ns, LANES), F32))
        idx_ref[0, ci * ns:(ci + 1) * ns, :] = out.astype(I32)


def _select(afft, cap):
    b, n_e, tn = afft.shape
    return pl.pallas_call(
        functools.partial(_select_kernel, cap),
        grid=(b,),
        in_specs=[pl.BlockSpec((1, n_e, tn), lambda bb: (bb, 0, 0))],
        out_specs=pl.BlockSpec((1, cap, LANES), lambda bb: (bb, 0, 0)),
        out_shape=jax.ShapeDtypeStruct((b, cap, LANES), I32),
        scratch_shapes=[pltpu.VMEM((n_e, tn), F32), pltpu.VMEM((n_e, LANES), I32), pltpu.SMEM((n_e, LANES), I32),
                        pltpu.SemaphoreType.DMA(())],
        compiler_params=_cparams("parallel"),
    )(afft)


def _expert_kernel(idx_ref, h_hbm, wg_ref, wu_ref, wd_ref, acc_in, acc_hbm,
                   stage, accst, xin, gate, yacc, sem_x, sem_a, sem_s):
    del acc_in
    e = pl.program_id(0)
    f = pl.program_id(1)
    n_e = pl.num_programs(0)
    nf, mb, dx = stage.shape
    m, d = xin.shape
    ns, ng = EXPERT_SCATTER_STEPS, EXPERT_GATHER_STEPS
    bs, bg = nf // ns, nf // ng
    base = e * m
    prev = base - m
    nxt = lax.rem(e + 1, n_e) * m

    def x_copy(first, blk, u):
        return pltpu.make_async_copy(h_hbm.at[pl.ds(idx_ref[first + blk * mb + u], 1)],
                                     stage.at[blk, pl.ds(u, 1)], sem_x)

    def a_copy(blk, u):
        return pltpu.make_async_copy(acc_hbm.at[pl.ds(idx_ref[base + blk * mb + u], 1)],
                                     accst.at[blk, pl.ds(u, 1)], sem_a)

    def s_copy(first, blk, u):
        return pltpu.make_async_copy(accst.at[blk, pl.ds(u, 1)],
                                     acc_hbm.at[pl.ds(idx_ref[first + blk * mb + u], 1)], sem_s)

    def looped(copy_of):
        for blk in range(nf):
            def body(i, carry):
                for u in range(DMA_UNROLL):
                    copy_of(blk, i * DMA_UNROLL + u).start()
                return carry

            lax.fori_loop(0, mb // DMA_UNROLL, body, 0)

    def wait_blocks(src_of, dst_of, sem):
        for blk in range(nf):
            pltpu.make_async_copy(src_of(blk), dst_of(blk), sem).wait()

    wait_x = lambda: wait_blocks(lambda blk: h_hbm.at[pl.ds(0, mb)], lambda blk: stage.at[blk], sem_x)
    wait_s = lambda: wait_blocks(lambda blk: accst.at[blk], lambda blk: acc_hbm.at[pl.ds(0, mb)], sem_s)

    @pl.when((e == 0) & (f == 0))
    def _():
        looped(lambda blk, u: x_copy(base, blk, u))
        wait_x()

    @pl.when(f == 0)
    def _():
        rows = stage[...]
        xin[...] = rows[:, :, :d].reshape(m, d).astype(BF16)
        lane = lax.broadcasted_iota(I32, (1, 1, LANES), 2)
        gate[...] = jnp.sum(jnp.where(lane == e, rows[:, :, d:], 0.0), axis=2, keepdims=True).reshape(m, 1)
        yacc[...] = jnp.zeros_like(yacc)

    @pl.when((f == ns) & (e > 0))
    def _():
        wait_s()

    def step(extra_copies):
        for u in range(mb):
            x_copy(nxt, f, u).start()
        extra_copies()
        x = xin[...]
        a = _dot(x, wg_ref[0, 0].astype(BF16))
        up = _dot(x, wu_ref[0, 0].astype(BF16))
        hm = (a * jax.nn.sigmoid(a) * up).astype(BF16)
        wd = wd_ref[0, 0].astype(BF16)
        rb = m // 4
        for r0 in range(0, m, rb):
            yacc[r0:r0 + rb] += _dot(hm[r0:r0 + rb], wd)

    def scatter_copies():
        for k in range(bs):
            for u in range(mb):
                s_copy(prev, f * bs + k, u).start()

    def gather_copies():
        for k in range(bg):
            for u in range(mb):
                a_copy((f - ns) * bg + k, u).start()

    scatters = (f < ns) & (e > 0)
    gathers = (f >= ns) & (f < ns + ng)
    pl.when(scatters)(lambda: step(scatter_copies))
    pl.when(gathers)(lambda: step(gather_copies))
    pl.when(jnp.logical_not(scatters | gathers))(lambda: step(lambda: None))

    @pl.when(f == nf - 1)
    def _():
        wait_blocks(lambda blk: acc_hbm.at[pl.ds(0, mb)], lambda blk: accst.at[blk], sem_a)
        accst[...] = accst[...] + (gate[...] * yacc[...]).reshape(nf, mb, d)
        wait_x()

    @pl.when((f == nf - 1) & (e == n_e - 1))
    def _():
        looped(lambda blk, u: s_copy(base, blk, u))
        wait_s()


def _experts(idx_flat, m, h_rows, acc, layer, w_gate, w_up, w_down):
    _, n_e, d, f = w_gate.shape
    tf = EXPERT_F_TILE
    nf = f // tf
    mb = m // nf
    assert mb * nf == m and mb % DMA_UNROLL == 0 and m % 64 == 0 and h_rows.shape[1] == d + LANES
    assert nf % EXPERT_SCATTER_STEPS == 0 and nf % EXPERT_GATHER_STEPS == 0
    assert EXPERT_SCATTER_STEPS + EXPERT_GATHER_STEPS <= nf
    grid_spec = pltpu.PrefetchScalarGridSpec(
        num_scalar_prefetch=1,
        grid=(n_e, f // tf),
        in_specs=[pl.BlockSpec(memory_space=pl.ANY),
                  pl.BlockSpec((1, 1, d, tf), lambda e, j, idx: (layer, e, 0, j)),
                  pl.BlockSpec((1, 1, d, tf), lambda e, j, idx: (layer, e, 0, j)),
                  pl.BlockSpec((1, 1, tf, d), lambda e, j, idx: (layer, e, j, 0)),
                  pl.BlockSpec(memory_space=pl.ANY)],
        out_specs=pl.BlockSpec(memory_space=pl.ANY),
        scratch_shapes=[pltpu.VMEM((nf, mb, d + LANES), F32), pltpu.VMEM((nf, mb, d), F32),
                        pltpu.VMEM((m, d), BF16), pltpu.VMEM((m, 1), F32), pltpu.VMEM((m, d), F32),
                        pltpu.SemaphoreType.DMA(()), pltpu.SemaphoreType.DMA(()), pltpu.SemaphoreType.DMA(())],
    )
    return pl.pallas_call(
        _expert_kernel,
        grid_spec=grid_spec,
        out_shape=jax.ShapeDtypeStruct(acc.shape, F32),
        input_output_aliases={5: 0},
        compiler_params=_cparams("arbitrary", "arbitrary"),
    )(idx_flat, h_rows, w_gate, w_up, w_down, acc)


def _moe_ln_kernel(x_ref, moe_ref, mod_ref, lnv_ref, o_ref):
    lnv = lnv_ref[...]
    m = mod_ref[0]
    o_ref[0] = _layer_norm(lnv[2:3] * x_ref[0] + m[5:6] * moe_ref[0], lnv[0:1], lnv[1:2])


def _moe_ln(x1, moe, mods, nctx_blk, lnv):
    b, r, d = x1.shape
    tm = ROW_TILE
    row = pl.BlockSpec((1, tm, d), lambda bb, i: (bb, i, 0))
    return pl.pallas_call(
        _moe_ln_kernel,
        grid=(b, r // tm),
        in_specs=[row, row,
                  pl.BlockSpec((1, N_MOD, d), lambda bb, i: (bb * 2 + (i >= nctx_blk).astype(I32), 0, 0)),
                  pl.BlockSpec(lnv.shape, lambda bb, i: (0, 0))],
        out_specs=row,
        out_shape=jax.ShapeDtypeStruct((b, r, d), F32),
        compiler_params=_cparams("parallel", "arbitrary"),
    )(x1, moe, mods, lnv)


def _ec_moe(h2x, afft, streams, layer, w_gate, w_up, w_down):
    b, r, dx = h2x.shape
    d = dx - LANES
    idx_parts = []
    for off, tn in streams:
        cap = EC_CAPACITY * tn // N_EXPERTS
        idx = _select(lax.slice_in_dim(afft, off, off + tn, axis=2), cap)[:, :, :N_EXPERTS]
        rows = idx + (jnp.arange(b, dtype=I32) * r + off)[:, None, None]
        idx_parts.append(jnp.transpose(rows, (2, 0, 1)).reshape(N_EXPERTS, b * cap))
    idx_flat = jnp.concatenate(idx_parts, axis=1)
    acc = _experts(idx_flat.reshape(-1), idx_flat.shape[1], h2x.reshape(b * r, dx),
                   jnp.zeros((b * r, d), F32), layer, w_gate, w_up, w_down)
    return acc.reshape(b, r, d)


def _qkv_kernel(x_ref, mod_ref, cos_ref, sin_ref, wq_ref, wk_ref, wv_ref, q_ref, k_ref, v_ref):
    m = mod_ref[0]
    h = (x_ref[0] * (1 + m[1:2]) + m[0:1]).astype(BF16)
    d = x_ref.shape[2]
    rep = d // LANES
    cos = jnp.concatenate([cos_ref[...]] * rep, axis=1)
    sin = jnp.concatenate([sin_ref[...]] * rep, axis=1)
    lane = lax.broadcasted_iota(I32, (1, d), 1)
    lower = (lane % 32) < 16

    def rope(t):
        partner = jnp.where(lower, pltpu.roll(t, d - 16, 1), pltpu.roll(t, 16, 1))
        return t * cos + partner * sin

    q_ref[0] = (rope(_dot(h, wq_ref[...])) * (HEAD ** -0.5)).astype(BF16)
    k_ref[0] = rope(_dot(h, wk_ref[...])).astype(BF16)
    v_ref[0] = _dot(h, wv_ref[...]).astype(BF16)


def _qkv(xa, mods, nctx_blk, cos, sin, wq, wk4, wv4):
    b, r, d = xa.shape
    tm = ROW_TILE
    row = pl.BlockSpec((1, tm, d), lambda bb, i: (bb, i, 0))
    tab = pl.BlockSpec((tm, LANES), lambda bb, i: (i, 0))
    full = lambda a: pl.BlockSpec(a.shape, lambda bb, i: (0,) * a.ndim)
    out = jax.ShapeDtypeStruct((b, r, d), BF16)
    return pl.pallas_call(
        _qkv_kernel,
        grid=(b, r // tm),
        in_specs=[row, pl.BlockSpec((1, N_MOD, d), lambda bb, i: (bb * 2 + (i >= nctx_blk).astype(I32), 0, 0)),
                  tab, tab, full(wq), full(wk4), full(wv4)],
        out_specs=[row] * 3,
        out_shape=[out] * 3,
        compiler_params=_cparams("parallel", "arbitrary"),
    )(xa, mods, cos, sin, wq, wk4, wv4)


def _attn_kernel(n_ctx, sink_ref, q_ref, kc_ref, vc_ref, kp_ref, k0_ref, kn_ref, vp_ref, v0_ref, vn_ref, o_ref):
    nb = pl.program_id(1)
    n_blk = pl.num_programs(1)
    bq = q_ref.shape[1]
    gw = ATT_GROUP * HEAD
    nk = n_ctx + 3 * bq
    col = lax.broadcasted_iota(I32, (bq, nk), 1)
    qpos = nb * bq + lax.broadcasted_iota(I32, (bq, nk), 0)
    kpos = (nb - 1) * bq + (col - n_ctx)
    valid = (col < n_ctx) | ((kpos >= 0) & (kpos < n_blk * bq) & (jnp.abs(qpos - kpos) <= WINDOW))
    lane = lax.broadcasted_iota(I32, (1, gw), 1)
    sls = [slice(gw * h, gw * (h + 1)) for h in range(ATT_KV_HEADS)]
    scores, probs = {}, {}

    def score_stage(h):
        sl = sls[h]
        keys = jnp.concatenate([kc_ref[0, :, sl], kp_ref[0, :, sl], k0_ref[0, :, sl], kn_ref[0, :, sl]], axis=0)
        qh = q_ref[0, :, sl]
        scores[h] = [_dot_nt(jnp.where((lane // HEAD) == g, qh, jnp.zeros_like(qh)), keys)
                     for g in range(ATT_GROUP)]

    def softmax_stage(h):
        probs[h] = []
        for g, s in enumerate(scores.pop(h)):
            s = jnp.where(valid, s, NEG)
            sink = sink_ref[h * ATT_GROUP + g]
            mx = jnp.maximum(jnp.max(s, axis=-1, keepdims=True), sink)
            p = jnp.exp(s - mx)
            den = jnp.sum(p, axis=-1, keepdims=True) + jnp.exp(sink - mx)
            probs[h].append((p.astype(BF16), 1.0 / den))

    def value_stage(h):
        sl = sls[h]
        vals = jnp.concatenate([vc_ref[0, :, sl], vp_ref[0, :, sl], v0_ref[0, :, sl], vn_ref[0, :, sl]], axis=0)
        acc = jnp.zeros((bq, gw), F32)
        for g, (p, inv) in enumerate(probs.pop(h)):
            acc = acc + jnp.where((lane // HEAD) == g, _dot(p, vals) * inv, 0.0)
        o_ref[0, :, sl] = acc.astype(BF16)

    for step in range(ATT_KV_HEADS + 2):
        if step < ATT_KV_HEADS:
            score_stage(step)
        if 1 <= step <= ATT_KV_HEADS:
            softmax_stage(step - 1)
        if step >= 2:
            value_stage(step - 2)


def _attention(q, k4, v4, sink, n_ctx):
    b, r, d = q.shape
    bq = WINDOW
    t = r - n_ctx
    nb = t // bq
    off = n_ctx // bq
    blk = lambda f: pl.BlockSpec((1, bq, d), f)
    ctx_spec = pl.BlockSpec((1, n_ctx, d), lambda bb, i: (bb, 0, 0))
    prev = lambda bb, i: (bb, jnp.maximum(i - 1, 0) + off, 0)
    cur = lambda bb, i: (bb, i + off, 0)
    nxt = lambda bb, i: (bb, jnp.minimum(i + 1, nb - 1) + off, 0)
    return pl.pallas_call(
        functools.partial(_attn_kernel, n_ctx),
        grid=(b, nb),
        in_specs=[pl.BlockSpec(memory_space=pltpu.SMEM), blk(cur), ctx_spec, ctx_spec,
                  blk(prev), blk(cur), blk(nxt), blk(prev), blk(cur), blk(nxt)],
        out_specs=pl.BlockSpec((1, bq, d), lambda bb, i: (bb, i, 0)),
        out_shape=jax.ShapeDtypeStruct((b, t, d), BF16),
        compiler_params=_cparams("parallel", "arbitrary"),
    )(sink, q, k4, v4, k4, k4, k4, v4, v4, v4)


def _rope_tables(n_ctx, t):
    half = HEAD // 2
    inv = 1.0 / (ROPE_BASE ** (jnp.arange(0, half, 2, dtype=F32) / half))
    pos = jnp.arange(t)
    ang_r = (pos // GRID_W).astype(F32)[:, None] * inv
    ang_c = (pos % GRID_W).astype(F32)[:, None] * inv
    cos = jnp.concatenate([jnp.cos(ang_r)] * 2 + [jnp.cos(ang_c)] * 2, axis=1)
    sin = jnp.concatenate([-jnp.sin(ang_r), jnp.sin(ang_r), -jnp.sin(ang_c), jnp.sin(ang_c)], axis=1)
    cos = jnp.concatenate([jnp.ones((n_ctx, HEAD), F32), cos], axis=0)
    sin = jnp.concatenate([jnp.zeros((n_ctx, HEAD), F32), sin], axis=0)
    return jnp.tile(cos, (1, 2)), jnp.tile(sin, (1, 2))


def kernel(x, c, ctx, c_ctx, ada_w, ada_b, ln1_g, ln1_b, ln2_g, ln2_b, rw_mu, rw_wr, rw_wk, rw_wv, rw_w0, rw_w1, rw_w2, rw_a0, rw_a1, rw_a2, rw_g1, rw_g2, rw_kk, rw_ka, rw_rk, rw_lnx_g, rw_lnx_b, rw_wo, at_wqkv, at_wo, at_sink, router_w, moe_w_gate, moe_w_up, moe_w_down):
    b, t, d = x.shape
    n_ctx = ctx.shape[1]
    depth = ada_w.shape[0]
    alpha = (2 * depth) ** 0.25
    assert n_ctx % ROW_TILE == 0 and t % ROW_TILE == 0 and b + 1 <= 8
    nctx_blk = n_ctx // ROW_TILE

    cc = jnp.concatenate([c, c_ctx[None], jnp.zeros((8 - b - 1, d), F32)], axis=0)
    mod_all = _ada_mod(cc, ada_w, ada_b).reshape(depth, 8, N_MOD, d)
    xa = jnp.concatenate([ctx, x], axis=1)
    bf = lambda w: w.astype(BF16)
    pad_rows = lambda w, lo: jnp.pad(w, ((lo, LANES - lo - w.shape[0]), (0, 0)))

    for i in range(depth):
        last = i == depth - 1
        j = i // 2
        mods = jnp.stack([jnp.broadcast_to(mod_all[i, b], (b, N_MOD, d)), mod_all[i, :b]], axis=1).reshape(2 * b, N_MOD, d)
        lnv1 = jnp.stack([ln1_g[i], ln1_b[i], jnp.full((d,), alpha, F32)])
        lnv2 = jnp.stack([ln2_g[i], ln2_b[i], jnp.full((d,), alpha, F32)])
        rw_pad = jnp.pad(router_w[i], ((0, 0), (0, LANES - N_EXPERTS)))
        if i % 2 == 0:
            vec = jnp.concatenate([rw_mu[j], rw_w0[j], rw_a0[j], rw_kk[j][None], rw_ka[j][None],
                                   rw_rk[j].reshape(1, d), jnp.zeros((3, d), F32)], axis=0)
            w1 = bf(jnp.concatenate([rw_w1[j, 0], rw_w1[j, 1]], axis=1))
            a1 = bf(jnp.concatenate([rw_a1[j, 0], rw_a1[j, 1]], axis=1))
            w2 = bf(jnp.stack([pad_rows(rw_w2[j, 0], 0), pad_rows(rw_w2[j, 1], HEAD)]))
            a2 = bf(jnp.stack([pad_rows(rw_a2[j, 0], 0), pad_rows(rw_a2[j, 1], HEAD)]))
            (r, v, kk, g, bonus, lw0, k0, e0, lw1, k1, e1) = _rwkv_proj(
                xa, mods, nctx_blk, vec, bf(rw_wr[j]), bf(rw_wk[j]), bf(rw_wv[j]), w1, a1, bf(rw_g1[j]),
                w2, a2, bf(rw_g2[j]))
            yf, yr = _rwkv_scan(r, v, kk, lw0, k0, e0, lw1, k1, e1, n_ctx // CHUNK)
            lnx = jnp.stack([rw_lnx_g[j], rw_lnx_b[j]])
            x1, h2, afft = _rwkv_out(yf, yr, bonus, g, xa, mods, nctx_blk, lnx, bf(rw_wo[j]), lnv1, rw_pad)
        else:
            nq = ATT_KV_HEADS * ATT_GROUP * HEAD
            nk = ATT_KV_HEADS * HEAD
            wqkv = at_wqkv[j]
            tile4 = lambda w: jnp.tile(w.reshape(d, ATT_KV_HEADS, 1, HEAD), (1, 1, ATT_GROUP, 1)).reshape(d, nq)
            cos, sin = _rope_tables(n_ctx, t)
            q, k4, v4 = _qkv(xa, mods, nctx_blk, cos, sin, bf(wqkv[:, :nq]),
                             bf(tile4(wqkv[:, nq:nq + nk])), bf(tile4(wqkv[:, nq + nk:])))
            o = _attention(q, k4, v4, at_sink[j], n_ctx)
            x1, h2, afft = _attn_out(o, xa, mods, nctx_blk, bf(at_wo[j]), lnv1, rw_pad)
        if last:
            if x1.shape[1] != t:
                x1, h2, afft = x1[:, n_ctx:], h2[:, n_ctx:], afft[:, :, n_ctx:]
            moe = _ec_moe(h2, afft, [(0, t)], i, moe_w_gate, moe_w_up, moe_w_down)
            return _moe_ln(x1, moe, mods, 0, lnv2)
        moe = _ec_moe(h2, afft, [(n_ctx, t), (0, n_ctx)], i, moe_w_gate, moe_w_up, moe_w_down)
        xa = _moe_ln(x1, moe, mods, nctx_blk, lnv2)
    return xa[:, n_ctx:]
```

```python
import functools

import jax
import jax.numpy as jnp
from jax import lax
from jax.experimental import pallas as pl
from jax.experimental.pallas import tpu as pltpu

F32 = jnp.float32
BF16 = jnp.bfloat16
I32 = jnp.int32
HIGHEST = lax.Precision.HIGHEST

LANES = 128
LN_EPS = 1e-5
N_MOD = 6
HEAD = 64
RW_LNX_EPS = 64e-5
RW_DECAY_SCALE = 0.606531
ATT_KV_HEADS = 4
ATT_GROUP = 4
WINDOW = 128
GRID_W = 64
ROPE_BASE = 10000.0
NEG = -1e30
N_EXPERTS = 16
EC_CAPACITY = 2
CHUNK = 64
DMA_UNROLL = 8
EXPERT_F_TILE = 256
EXPERT_SCATTER_STEPS = 2
EXPERT_GATHER_STEPS = 4
ROW_TILE = 256
VMEM_LIMIT = 56 * 2 ** 20


def _cparams(*sem):
    return pltpu.CompilerParams(dimension_semantics=sem, vmem_limit_bytes=VMEM_LIMIT)


def _dot(a, b):
    return jnp.dot(a, b, preferred_element_type=F32)


def _dot_nt(a, b):
    return lax.dot_general(a, b, (((1,), (1,)), ((), ())), preferred_element_type=F32)


def _split(x):
    hi = x.astype(BF16)
    lo = (x - hi.astype(F32)).astype(BF16)
    return hi, lo


def _head_sum(x):
    d = x.shape[-1]
    gr = (lax.broadcasted_iota(I32, (d, LANES), 0) // HEAD
          == lax.broadcasted_iota(I32, (d, LANES), 1)).astype(BF16)
    ge = (lax.broadcasted_iota(I32, (LANES, d), 0)
          == lax.broadcasted_iota(I32, (LANES, d), 1) // HEAD).astype(BF16)
    s = _dot(jnp.concatenate(_split(x), axis=1), jnp.concatenate([gr, gr], axis=0))
    return _dot(jnp.concatenate(_split(s), axis=1), jnp.concatenate([ge, ge], axis=0))


def _layer_norm(x, g, b):
    mu = jnp.mean(x, axis=-1, keepdims=True)
    xc = x - mu
    var = jnp.mean(xc * xc, axis=-1, keepdims=True)
    return xc * lax.rsqrt(var + LN_EPS) * g + b


def _ada_kernel(cc_ref, w_ref, b_ref, o_ref):
    cc = cc_ref[...]
    s = cc * jax.nn.sigmoid(cc)
    o_ref[0] = jnp.dot(s, w_ref[0], precision=HIGHEST, preferred_element_type=F32) + b_ref[0]


def _ada_mod(cc, ada_w, ada_b):
    depth, d, n = ada_w.shape
    tn = n // 4
    return pl.pallas_call(
        _ada_kernel,
        grid=(depth, n // tn),
        in_specs=[pl.BlockSpec((8, d), lambda l, j: (0, 0)),
                  pl.BlockSpec((1, d, tn), lambda l, j: (l, 0, j)),
                  pl.BlockSpec((1, 1, tn), lambda l, j: (l, 0, j))],
        out_specs=pl.BlockSpec((1, 8, tn), lambda l, j: (l, 0, j)),
        out_shape=jax.ShapeDtypeStruct((depth, 8, n), F32),
        compiler_params=_cparams("arbitrary", "arbitrary"),
    )(cc, ada_w, ada_b.reshape(depth, 1, n))


def _rwkv_proj_kernel(nctx_blk, c_ref, x_ref, xp_ref, xn_ref, mod_ref, vec_ref, wr_ref, wk_ref, wv_ref,
                      w1_ref, a1_ref, g1_ref, w2_ref, a2_ref, g2_ref,
                      r_ref, v_ref, kk_ref, g_ref, bonus_ref,
                      lw0_ref, k0_ref, e0_ref, lw1_ref, k1_ref, e1_ref):
    i = pl.program_id(1)
    nblk = pl.num_programs(1)
    tm = x_ref.shape[1]
    m = mod_ref[0]
    shift, scale = m[0:1], m[1:2]
    h = jnp.where(i < nctx_blk, c_ref[0], x_ref[0]) * (1 + scale) + shift
    first = (i == 0) | (i == nctx_blk)
    last = (i == nctx_blk - 1) | (i == nblk - 1)
    hp = jnp.where(first, 0.0, xp_ref[0][7:8] * (1 + scale) + shift)
    hn = jnp.where(last, 0.0, xn_ref[0][0:1] * (1 + scale) + shift)
    rows = lax.broadcasted_iota(I32, (tm, 1), 0)
    prev = jnp.where(rows == 0, hp, pltpu.roll(h, 1, 0))
    nxt = jnp.where(rows == tm - 1, hn, pltpu.roll(h, tm - 1, 0))
    xx = 0.5 * (prev + nxt) - h
    vec = vec_ref[...]
    mix = lambda j: (h + xx * vec[j:j + 1]).astype(BF16)
    r = _dot(mix(0), wr_ref[...])
    k = _dot(mix(2), wk_ref[...])
    v = _dot(mix(3), wv_ref[...])
    tw = jnp.tanh(_dot(mix(1), w1_ref[...])).astype(BF16)
    ta = _dot(mix(4), a1_ref[...]).astype(BF16)
    g = _dot(jax.nn.sigmoid(_dot(mix(5), g1_ref[...])).astype(BF16), g2_ref[...])
    kk = k * vec[10:11]
    nrm = jnp.sqrt(_head_sum(kk * kk))
    kk = kk / jnp.maximum(nrm, 1e-12)
    bonus = _head_sum(r * k * vec[12:13]) * v
    r_ref[0] = r
    v_ref[0] = v
    kk_ref[0] = kk
    g_ref[0] = g
    bonus_ref[0] = bonus
    for d, (lw_ref, kd_ref, e_ref) in enumerate(((lw0_ref, k0_ref, e0_ref), (lw1_ref, k1_ref, e1_ref))):
        lw_ref[0] = -RW_DECAY_SCALE * jax.nn.sigmoid(vec[6 + d:7 + d] + _dot(tw, w2_ref[d]))
        eta = jax.nn.sigmoid(vec[8 + d:9 + d] + _dot(ta, a2_ref[d]))
        e_ref[0] = eta
        kd_ref[0] = k * (1 + (eta - 1) * vec[11:12])


def _rwkv_proj(ctx, x, mods, vec, wr, wk, wv, w1, a1, g1, w2, a2, g2):
    b, t, d = x.shape
    tm = ROW_TILE
    assert ctx.shape[1] == tm
    nctx_blk = 1
    r = ctx.shape[1] + t
    nblk = r // tm
    hb = tm // 8
    row = pl.BlockSpec((1, tm, d), lambda bb, i: (bb, i, 0))
    lat = lambda i: jnp.maximum(i - nctx_blk, 0)
    full = lambda a: pl.BlockSpec(a.shape, lambda bb, i: (0,) * a.ndim)
    out = jax.ShapeDtypeStruct((b, r, d), F32)
    return pl.pallas_call(
        functools.partial(_rwkv_proj_kernel, nctx_blk),
        grid=(b, nblk),
        in_specs=[pl.BlockSpec((1, tm, d), lambda bb, i: (bb, 0, 0)),
                  pl.BlockSpec((1, tm, d), lambda bb, i: (bb, lat(i), 0)),
                  pl.BlockSpec((1, 8, d), lambda bb, i: (bb, jnp.maximum(lat(i) * hb - 1, 0), 0)),
                  pl.BlockSpec((1, 8, d), lambda bb, i: (bb, jnp.minimum((lat(i) + 1) * hb, t // 8 - 1), 0)),
                  pl.BlockSpec((1, N_MOD, d), lambda bb, i: (bb * 2 + (i >= nctx_blk).astype(I32), 0, 0)),
                  full(vec), full(wr), full(wk), full(wv), full(w1), full(a1), full(g1),
                  full(w2), full(a2), full(g2)],
        out_specs=[row] * 11,
        out_shape=[out] * 11,
        compiler_params=_cparams("parallel", "arbitrary"),
    )(ctx, x, x, x, mods, vec, wr, wk, wv, w1, a1, g1, w2, a2, g2)


def _scan_kernel(rf_ref, vf_ref, kkf_ref, lw0_ref, k0_ref, e0_ref,
                 rr_ref, vr_ref, kkr_ref, lw1_ref, k1_ref, e1_ref,
                 yf_ref, yr_ref, s_ref):
    c = CHUNK
    assert c == HEAD

    @pl.when(pl.program_id(1) == 0)
    def _():
        s_ref[...] = jnp.zeros_like(s_ref)

    ti = lax.broadcasted_iota(I32, (c, c), 0)
    tj = lax.broadcasted_iota(I32, (c, c), 1)
    tt = lax.broadcasted_iota(I32, (c, 2 * c), 0)
    ss = lax.broadcasted_iota(I32, (c, 2 * c), 1) % c
    m0 = lax.broadcasted_iota(I32, (1, 2 * HEAD), 1) < HEAD

    def stack(x):
        return jnp.concatenate([jnp.where(m0, x, 0.0), jnp.where(m0, 0.0, x)], axis=0)

    dirs = ((rf_ref, vf_ref, kkf_ref, lw0_ref, k0_ref, e0_ref, yf_ref),
            (rr_ref, vr_ref, kkr_ref, lw1_ref, k1_ref, e1_ref, yr_ref))
    n_pair = s_ref.shape[1]
    ch = []
    for d, (r_ref, v_ref, kk_ref, lw_ref, kd_ref, e_ref, y_ref) in enumerate(dirs):
        if d == 0:
            tri = (tj <= ti).astype(BF16)
            strict, incl = ss < tt, ss <= tt
        else:
            tri = (tj >= ti).astype(BF16)
            strict, incl = ss > tt, ss >= tt
        lw = lw_ref[0]
        lw_hi = lw.astype(BF16)
        lw_mid = (lw - lw_hi.astype(F32)).astype(BF16)
        lw_lo = (lw - lw_hi.astype(F32) - lw_mid.astype(F32)).astype(BF16)
        cs = _dot(jnp.concatenate([tri, tri, tri], axis=1), jnp.concatenate([lw_hi, lw_mid, lw_lo], axis=0))
        gam = jnp.exp(cs)
        gam_inv = jnp.exp(-cs)
        kk = kk_ref[0]
        a_t = -(kk * jnp.exp(cs - lw))
        b_t = kk * e_ref[0] * gam_inv
        k_t = kd_ref[0] * gam_inv
        r_t = r_ref[0] * gam
        v_all = v_ref[0]
        g_end = gam[c - 1:c] if d == 0 else gam[0:1]
        for p in range(n_pair):
            sl = slice(2 * HEAD * p, 2 * HEAD * (p + 1))
            ch.append(dict(
                d=d, p=p, sl=sl, y_ref=y_ref, strict=strict, incl=incl, v2=stack(v_all[:, sl]),
                ar=jnp.concatenate([a_t[:, sl], r_t[:, sl]], axis=0).astype(BF16),
                bk=jnp.concatenate([stack(b_t[:, sl]), stack(k_t[:, sl])], axis=0),
                ge=g_end[:, sl], s0=s_ref[d, p]))
    for q in ch:
        q["m1"] = _dot_nt(q["ar"], q["bk"].astype(BF16))
    for q in ch:
        q["ars"] = _dot_nt(q["ar"], q["s0"].astype(BF16))
    for q in ch:
        m1 = q.pop("m1")
        q["pw"] = jnp.where(q["strict"], m1[:c, :2 * c], 0.0)
        lak = jnp.where(q["strict"], m1[:c, 2 * c:], 0.0).astype(BF16)
        q["lr"] = jnp.concatenate([jnp.where(q["incl"], m1[c:, :2 * c], 0.0),
                                   jnp.where(q["incl"], m1[c:, 2 * c:], 0.0)], axis=1).astype(BF16)
        q["x"] = q["ars"][:c] + _dot(lak, q["v2"].astype(BF16))
    for it in range(6):
        for q in ch:
            pb = q["pw"].astype(BF16)
            if it < 5:
                px = _dot(pb, jnp.concatenate([stack(q["x"]), stack(q["pw"])], axis=1).astype(BF16))
                q["x"] = q["x"] + px[:, :2 * c]
                q["pw"] = px[:, 2 * c:]
            else:
                q["x"] = q["x"] + _dot(pb, stack(q["x"]).astype(BF16))
    for q in ch:
        q["uv"] = jnp.concatenate([stack(q["x"]), q["v2"]], axis=0)
        q["y_ref"][0, :, q["sl"]] = q["ars"][c:] + _dot(q["lr"], q["uv"].astype(BF16))
    for q in ch:
        s_ref[q["d"], q["p"]] = (q["s0"] * q["ge"]
                                 + _dot(q["uv"].T.astype(BF16), (q["bk"] * q["ge"]).astype(BF16)))


def _rwkv_scan(r, v, kk, lw0, k0, e0, lw1, k1, e1, nctx_chunks):
    b, rows, d = r.shape
    c = CHUNK
    nc = rows // c
    fwd = lambda bb, i: (bb, i, 0)
    rev = lambda bb, i: (bb, jnp.where(i < nctx_chunks, nctx_chunks - 1 - i, nc - 1 - (i - nctx_chunks)), 0)
    sf = pl.BlockSpec((1, c, d), fwd)
    sr = pl.BlockSpec((1, c, d), rev)
    out = jax.ShapeDtypeStruct((b, rows, d), F32)
    return pl.pallas_call(
        _scan_kernel,
        grid=(b, nc),
        in_specs=[sf] * 6 + [sr] * 6,
        out_specs=[sf, sr],
        out_shape=[out, out],
        scratch_shapes=[pltpu.VMEM((2, d // (2 * HEAD), 2 * HEAD, 2 * HEAD), F32)],
        compiler_params=_cparams("parallel", "arbitrary"),
    )(r, v, kk, lw0, k0, e0, r, v, kk, lw1, k1, e1)


def _post_mixer(z, x_ref, mod_ref, stream, wo_ref, lnv, rw_ref, x1_ref, h2_ref, afft_ref):
    n, tm, d = x_ref.shape
    o = _dot(z.astype(BF16), wo_ref[...])
    x1s, h2s = [], []
    for s in range(n):
        m = mod_ref[2 * s + stream]
        x1 = _layer_norm(lnv[2:3] * x_ref[s] + m[2:3] * o[s * tm:(s + 1) * tm], lnv[0:1], lnv[1:2])
        x1s.append(x1)
        h2s.append(x1 * (1 + m[4:5]) + m[3:4])
    h_hi, h_lo = _split(jnp.concatenate(h2s, axis=0))
    logits = _dot(jnp.concatenate([h_hi, h_lo, h_hi], axis=1), rw_ref[...])
    lane = lax.broadcasted_iota(I32, logits.shape, 1)
    logits = jnp.where(lane < N_EXPERTS, logits, NEG)
    ex = jnp.exp(logits - jnp.max(logits, axis=-1, keepdims=True))
    aff = ex / jnp.sum(ex, axis=-1, keepdims=True)
    for s in range(n):
        aff_s = aff[s * tm:(s + 1) * tm]
        x1_ref[s] = x1s[s]
        h2_ref[s, :, :d] = h2s[s]
        h2_ref[s, :, d:] = aff_s
        afft_ref[s] = aff_s.T[:N_EXPERTS]


def _rwkv_out_kernel(nctx_blk, yf_ref, yr_ref, bonus_ref, g_ref, c_ref, x_ref, mod_ref, lnx_ref, wo_ref, lnv_ref,
                     rw_ref, x1_ref, h2_ref, afft_ref):
    n, tm, d = x_ref.shape
    rows = lambda ref: ref[...].reshape(n * tm, d)
    y = rows(yf_ref) + rows(yr_ref)
    mu = _head_sum(y) * (1.0 / HEAD)
    yc = y - mu
    var = _head_sum(yc * yc) * (1.0 / HEAD)
    lnx = lnx_ref[...]
    yn = yc * lax.rsqrt(var + RW_LNX_EPS) * lnx[0:1] + lnx[1:2]
    z = (yn + rows(bonus_ref)) * rows(g_ref)
    is_lat = pl.program_id(0) >= nctx_blk
    xs = jnp.where(is_lat, x_ref[...], c_ref[...])
    _post_mixer(z, xs, mod_ref, is_lat.astype(I32), wo_ref, lnv_ref[...], rw_ref, x1_ref, h2_ref, afft_ref)


def _post_mixer_outs(b, rows, d, tm):
    specs = [pl.BlockSpec((b, tm, d), lambda i: (0, i, 0)),
             pl.BlockSpec((b, tm, d + LANES), lambda i: (0, i, 0)),
             pl.BlockSpec((b, N_EXPERTS, tm), lambda i: (0, 0, i))]
    shapes = [jax.ShapeDtypeStruct((b, rows, d), F32), jax.ShapeDtypeStruct((b, rows, d + LANES), F32),
              jax.ShapeDtypeStruct((b, N_EXPERTS, rows), F32)]
    return specs, shapes


def _rwkv_out(yf, yr, bonus, g, ctx, x, mods, lnx, wo, lnv, rw_pad):
    b, r, d = yf.shape
    tm = ROW_TILE
    assert ctx.shape[1] == tm
    nctx_blk = 1
    row = pl.BlockSpec((b, tm, d), lambda i: (0, i, 0))
    full = lambda a: pl.BlockSpec(a.shape, lambda i: (0,) * a.ndim)
    out_specs, out_shape = _post_mixer_outs(b, r, d, tm)
    return pl.pallas_call(
        functools.partial(_rwkv_out_kernel, nctx_blk),
        grid=(r // tm,),
        in_specs=[row] * 4 + [pl.BlockSpec((b, tm, d), lambda i: (0, 0, 0)),
                              pl.BlockSpec((b, tm, d), lambda i: (0, jnp.maximum(i - nctx_blk, 0), 0)),
                              full(mods), full(lnx), full(wo), full(lnv), full(rw_pad)],
        out_specs=out_specs,
        out_shape=out_shape,
        compiler_params=_cparams("arbitrary"),
    )(yf, yr, bonus, g, ctx, x, mods, lnx, wo, lnv, rw_pad)


def _attn_out_kernel(o_ref, x_ref, mod_ref, wo_ref, lnv_ref, rw_ref, x1_ref, h2_ref, afft_ref):
    n, tm, d = x_ref.shape
    _post_mixer(o_ref[...].reshape(n * tm, d), x_ref, mod_ref, 1, wo_ref, lnv_ref[...], rw_ref,
                x1_ref, h2_ref, afft_ref)


def _attn_out(o, xa, mods, blk_off, wo, lnv, rw_pad):
    b, t, d = o.shape
    tm = ROW_TILE
    full = lambda a: pl.BlockSpec(a.shape, lambda i: (0,) * a.ndim)
    out_specs, out_shape = _post_mixer_outs(b, t, d, tm)
    return pl.pallas_call(
        _attn_out_kernel,
        grid=(t // tm,),
        in_specs=[pl.BlockSpec((b, tm, d), lambda i: (0, i, 0)),
                  pl.BlockSpec((b, tm, d), lambda i: (0, i + blk_off, 0)),
                  full(mods), full(wo), full(lnv), full(rw_pad)],
        out_specs=out_specs,
        out_shape=out_shape,
        compiler_params=_cparams("arbitrary"),
    )(o, xa, mods, wo, lnv, rw_pad)


def _select_kernel(cap, aff_ref, idx_ref, c_ref, bnd_v, bnd_s, sem):
    n_e, tn = aff_ref.shape[1], aff_ref.shape[2]

    def bit_step(j, bits):
        cand = bits | jnp.left_shift(jnp.int32(1), 30 - j)
        cnt = jnp.sum((aff_ref[0] >= lax.bitcast_convert_type(cand, F32)).astype(F32), axis=1, keepdims=True)
        return jnp.where(cnt >= cap, cand, bits)

    bits = lax.fori_loop(0, 31, bit_step, jnp.zeros((n_e, 1), I32))
    thr = lax.bitcast_convert_type(bits, F32)
    need = cap - jnp.sum((aff_ref[0] > thr).astype(F32), axis=1, keepdims=True)
    triu = (lax.broadcasted_iota(I32, (LANES, LANES), 0) <= lax.broadcasted_iota(I32, (LANES, LANES), 1)).astype(BF16)
    ns = min(cap, 256)
    tc = min(tn, 1024)
    lane = lax.broadcasted_iota(I32, (1, LANES), 1)
    tie_off = jnp.zeros((n_e, 1), F32)
    sel_off = jnp.zeros((n_e, 1), F32)
    bnd = jnp.zeros((n_e, LANES), F32)
    for j in range(tn // LANES):
        cols = slice(j * LANES, (j + 1) * LANES)
        a = aff_ref[0, :, cols]
        eq = (a == thr).astype(F32)
        tie_rank = _dot(eq.astype(BF16), triu) + tie_off - eq
        sel = ((a > thr) | ((eq > 0) & (tie_rank < need))).astype(F32)
        c_ref[:, cols] = _dot(sel.astype(BF16), triu) + sel_off
        tie_off = tie_off + jnp.sum(eq, axis=1, keepdims=True)
        sel_off = sel_off + jnp.sum(sel, axis=1, keepdims=True)
        if (j + 1) % (tc // LANES) == 0:
            bnd = jnp.where(lane == (j + 1) // (tc // LANES), sel_off, bnd)
    bnd_v[...] = bnd.astype(I32)
    to_smem = pltpu.make_async_copy(bnd_v, bnd_s, sem)
    to_smem.start()
    to_smem.wait()

    for ci in range(cap // ns):
        slot = (lax.broadcasted_iota(I32, (ns, 1), 0) + ci * ns).astype(F32)

        def per_expert(e, out):
            def chunk(j, part):
                def compare():
                    ce = c_ref[pl.ds(e, 1), pl.ds(pl.multiple_of(j * tc, tc), tc)]
                    acc = part
                    for k in range(tc // LANES):
                        acc = acc + (ce[:, k * LANES:(k + 1) * LANES] <= slot).astype(F32)
                    return acc

                before, upto = bnd_s[e, j], bnd_s[e, j + 1]
                return lax.cond(upto <= ci * ns, lambda: part + float(tc // LANES),
                                lambda: lax.cond(before > ci * ns + ns - 1, lambda: part, compare))

            part = lax.fori_loop(0, tn // tc, chunk, jnp.zeros((ns, LANES), F32))
            return jnp.where(lane == e, jnp.sum(part, axis=1, keepdims=True), out)

        out = lax.fori_loop(0, n_e, per_expert, jnp.zeros((ns, LANES), F32))
        idx_ref[0, ci * ns:(ci + 1) * ns, :] = out.astype(I32)


def _select(afft, cap):
    b, n_e, tn = afft.shape
    return pl.pallas_call(
        functools.partial(_select_kernel, cap),
        grid=(b,),
        in_specs=[pl.BlockSpec((1, n_e, tn), lambda bb: (bb, 0, 0))],
        out_specs=pl.BlockSpec((1, cap, LANES), lambda bb: (bb, 0, 0)),
        out_shape=jax.ShapeDtypeStruct((b, cap, LANES), I32),
        scratch_shapes=[pltpu.VMEM((n_e, tn), F32), pltpu.VMEM((n_e, LANES), I32), pltpu.SMEM((n_e, LANES), I32),
                        pltpu.SemaphoreType.DMA(())],
        compiler_params=_cparams("parallel"),
    )(afft)


def _expert_kernel(idx_ref, h_hbm, wg_ref, wu_ref, wd_ref, acc_in, acc_hbm,
                   stage, accst, xin, gate, yacc, sem_x, sem_a, sem_s):
    del acc_in
    e = pl.program_id(0)
    f = pl.program_id(1)
    n_e = pl.num_programs(0)
    nf, mb, dx = stage.shape
    m, d = xin.shape
    ns, ng = EXPERT_SCATTER_STEPS, EXPERT_GATHER_STEPS
    bs, bg = nf // ns, nf // ng
    base = e * m
    prev = base - m
    nxt = lax.rem(e + 1, n_e) * m

    def x_copy(first, blk, u):
        return pltpu.make_async_copy(h_hbm.at[pl.ds(idx_ref[first + blk * mb + u], 1)],
                                     stage.at[blk, pl.ds(u, 1)], sem_x)

    def a_copy(blk, u):
        return pltpu.make_async_copy(acc_hbm.at[pl.ds(idx_ref[base + blk * mb + u], 1)],
                                     accst.at[blk, pl.ds(u, 1)], sem_a)

    def s_copy(first, blk, u):
        return pltpu.make_async_copy(accst.at[blk, pl.ds(u, 1)],
                                     acc_hbm.at[pl.ds(idx_ref[first + blk * mb + u], 1)], sem_s)

    def looped(copy_of):
        for blk in range(nf):
            def body(i, carry):
                for u in range(DMA_UNROLL):
                    copy_of(blk, i * DMA_UNROLL + u).start()
                return carry

            lax.fori_loop(0, mb // DMA_UNROLL, body, 0)

    def wait_blocks(src_of, dst_of, sem):
        for blk in range(nf):
            pltpu.make_async_copy(src_of(blk), dst_of(blk), sem).wait()

    wait_x = lambda: wait_blocks(lambda blk: h_hbm.at[pl.ds(0, mb)], lambda blk: stage.at[blk], sem_x)
    wait_s = lambda: wait_blocks(lambda blk: accst.at[blk], lambda blk: acc_hbm.at[pl.ds(0, mb)], sem_s)

    @pl.when((e == 0) & (f == 0))
    def _():
        looped(lambda blk, u: x_copy(base, blk, u))
        wait_x()

    @pl.when(f == 0)
    def _():
        rows = stage[...]
        xin[...] = rows[:, :, :d].reshape(m, d).astype(BF16)
        lane = lax.broadcasted_iota(I32, (1, 1, LANES), 2)
        gate[...] = jnp.sum(jnp.where(lane == e, rows[:, :, d:], 0.0), axis=2, keepdims=True).reshape(m, 1)
        yacc[...] = jnp.zeros_like(yacc)

    @pl.when((f == ns) & (e > 0))
    def _():
        wait_s()

    def step(extra_copies):
        for u in range(mb):
            x_copy(nxt, f, u).start()
        extra_copies()
        x = xin[...]
        a = _dot(x, wg_ref[0, 0].astype(BF16))
        up = _dot(x, wu_ref[0, 0].astype(BF16))
        hm = (a * jax.nn.sigmoid(a) * up).astype(BF16)
        wd = wd_ref[0, 0].astype(BF16)
        rb = m // 4
        for r0 in range(0, m, rb):
            yacc[r0:r0 + rb] += _dot(hm[r0:r0 + rb], wd)

    def scatter_copies():
        for k in range(bs):
            for u in range(mb):
                s_copy(prev, f * bs + k, u).start()

    def gather_copies():
        for k in range(bg):
            for u in range(mb):
                a_copy((f - ns) * bg + k, u).start()

    scatters = (f < ns) & (e > 0)
    gathers = (f >= ns) & (f < ns + ng)
    pl.when(scatters)(lambda: step(scatter_copies))
    pl.when(gathers)(lambda: step(gather_copies))
    pl.when(jnp.logical_not(scatters | gathers))(lambda: step(lambda: None))

    @pl.when(f == nf - 1)
    def _():
        wait_blocks(lambda blk: acc_hbm.at[pl.ds(0, mb)], lambda blk: accst.at[blk], sem_a)
        accst[...] = accst[...] + (gate[...] * yacc[...]).reshape(nf, mb, d)
        wait_x()

    @pl.when((f == nf - 1) & (e == n_e - 1))
    def _():
        looped(lambda blk, u: s_copy(base, blk, u))
        wait_s()


def _experts(idx_flat, m, h_rows, acc, layer, w_gate, w_up, w_down):
    _, n_e, d, f = w_gate.shape
    tf = EXPERT_F_TILE
    nf = f // tf
    mb = m // nf
    assert mb * nf == m and mb % DMA_UNROLL == 0 and m % 64 == 0 and h_rows.shape[1] == d + LANES
    assert nf % EXPERT_SCATTER_STEPS == 0 and nf % EXPERT_GATHER_STEPS == 0
    assert EXPERT_SCATTER_STEPS + EXPERT_GATHER_STEPS <= nf
    grid_spec = pltpu.PrefetchScalarGridSpec(
        num_scalar_prefetch=1,
        grid=(n_e, f // tf),
        in_specs=[pl.BlockSpec(memory_space=pl.ANY),
                  pl.BlockSpec((1, 1, d, tf), lambda e, j, idx: (layer, e, 0, j)),
                  pl.BlockSpec((1, 1, d, tf), lambda e, j, idx: (layer, e, 0, j)),
                  pl.BlockSpec((1, 1, tf, d), lambda e, j, idx: (layer, e, j, 0)),
                  pl.BlockSpec(memory_space=pl.ANY)],
        out_specs=pl.BlockSpec(memory_space=pl.ANY),
        scratch_shapes=[pltpu.VMEM((nf, mb, d + LANES), F32), pltpu.VMEM((nf, mb, d), F32),
                        pltpu.VMEM((m, d), BF16), pltpu.VMEM((m, 1), F32), pltpu.VMEM((m, d), F32),
                        pltpu.SemaphoreType.DMA(()), pltpu.SemaphoreType.DMA(()), pltpu.SemaphoreType.DMA(())],
    )
    return pl.pallas_call(
        _expert_kernel,
        grid_spec=grid_spec,
        out_shape=jax.ShapeDtypeStruct(acc.shape, F32),
        input_output_aliases={5: 0},
        compiler_params=_cparams("arbitrary", "arbitrary"),
    )(idx_flat, h_rows, w_gate, w_up, w_down, acc)


def _moe_ln_kernel(x_ref, moe_ref, mod_ref, lnv_ref, o_ref):
    lnv = lnv_ref[...]
    m = mod_ref[0]
    o_ref[0] = _layer_norm(lnv[2:3] * x_ref[0] + m[5:6] * moe_ref[0], lnv[0:1], lnv[1:2])


def _moe_ln(x1, moe, mods, nctx_blk, lnv):
    b, r, d = x1.shape
    tm = ROW_TILE
    row = pl.BlockSpec((1, tm, d), lambda bb, i: (bb, i, 0))
    return pl.pallas_call(
        _moe_ln_kernel,
        grid=(b, r // tm),
        in_specs=[row, row,
                  pl.BlockSpec((1, N_MOD, d), lambda bb, i: (bb * 2 + (i >= nctx_blk).astype(I32), 0, 0)),
                  pl.BlockSpec(lnv.shape, lambda bb, i: (0, 0))],
        out_specs=row,
        out_shape=jax.ShapeDtypeStruct((b, r, d), F32),
        compiler_params=_cparams("parallel", "arbitrary"),
    )(x1, moe, mods, lnv)


def _ec_moe(h2x, afft, streams, layer, w_gate, w_up, w_down):
    b, r, dx = h2x.shape
    d = dx - LANES
    idx_parts = []
    for off, tn in streams:
        cap = EC_CAPACITY * tn // N_EXPERTS
        idx = _select(lax.slice_in_dim(afft, off, off + tn, axis=2), cap)[:, :, :N_EXPERTS]
        rows = idx + (jnp.arange(b, dtype=I32) * r + off)[:, None, None]
        idx_parts.append(jnp.transpose(rows, (2, 0, 1)).reshape(N_EXPERTS, b * cap))
    idx_flat = jnp.concatenate(idx_parts, axis=1)
    acc = _experts(idx_flat.reshape(-1), idx_flat.shape[1], h2x.reshape(b * r, dx),
                   jnp.zeros((b * r, d), F32), layer, w_gate, w_up, w_down)
    return acc.reshape(b, r, d)


def _qkv_kernel(x_ref, mod_ref, cos_ref, sin_ref, wq_ref, wk_ref, wv_ref, q_ref, k_ref, v_ref):
    m = mod_ref[0]
    h = (x_ref[0] * (1 + m[1:2]) + m[0:1]).astype(BF16)
    d = x_ref.shape[2]
    rep = d // LANES
    cos = jnp.concatenate([cos_ref[...]] * rep, axis=1)
    sin = jnp.concatenate([sin_ref[...]] * rep, axis=1)
    lane = lax.broadcasted_iota(I32, (1, d), 1)
    lower = (lane % 32) < 16

    def rope(t):
        partner = jnp.where(lower, pltpu.roll(t, d - 16, 1), pltpu.roll(t, 16, 1))
        return t * cos + partner * sin

    q_ref[0] = (rope(_dot(h, wq_ref[...])) * (HEAD ** -0.5)).astype(BF16)
    k_ref[0] = rope(_dot(h, wk_ref[...])).astype(BF16)
    v_ref[0] = _dot(h, wv_ref[...]).astype(BF16)


def _qkv(xa, mods, nctx_blk, cos, sin, wq, wk4, wv4):
    b, r, d = xa.shape
    tm = ROW_TILE
    row = pl.BlockSpec((1, tm, d), lambda bb, i: (bb, i, 0))
    tab = pl.BlockSpec((tm, LANES), lambda bb, i: (i, 0))
    full = lambda a: pl.BlockSpec(a.shape, lambda bb, i: (0,) * a.ndim)
    out = jax.ShapeDtypeStruct((b, r, d), BF16)
    return pl.pallas_call(
        _qkv_kernel,
        grid=(b, r // tm),
        in_specs=[row, pl.BlockSpec((1, N_MOD, d), lambda bb, i: (bb * 2 + (i >= nctx_blk).astype(I32), 0, 0)),
                  tab, tab, full(wq), full(wk4), full(wv4)],
        out_specs=[row] * 3,
        out_shape=[out] * 3,
        compiler_params=_cparams("parallel", "arbitrary"),
    )(xa, mods, cos, sin, wq, wk4, wv4)


def _attn_kernel(n_ctx, sink_ref, q_ref, kc_ref, vc_ref, kp_ref, k0_ref, kn_ref, vp_ref, v0_ref, vn_ref, o_ref):
    nb = pl.program_id(1)
    n_blk = pl.num_programs(1)
    bq = q_ref.shape[1]
    gw = ATT_GROUP * HEAD
    nk = n_ctx + 3 * bq
    col = lax.broadcasted_iota(I32, (bq, nk), 1)
    qpos = nb * bq + lax.broadcasted_iota(I32, (bq, nk), 0)
    kpos = (nb - 1) * bq + (col - n_ctx)
    valid = (col < n_ctx) | ((kpos >= 0) & (kpos < n_blk * bq) & (jnp.abs(qpos - kpos) <= WINDOW))
    lane = lax.broadcasted_iota(I32, (1, gw), 1)
    sls = [slice(gw * h, gw * (h + 1)) for h in range(ATT_KV_HEADS)]
    scores, probs = {}, {}

    def score_stage(h):
        sl = sls[h]
        keys = jnp.concatenate([kc_ref[0, :, sl], kp_ref[0, :, sl], k0_ref[0, :, sl], kn_ref[0, :, sl]], axis=0)
        qh = q_ref[0, :, sl]
        scores[h] = [_dot_nt(jnp.where((lane // HEAD) == g, qh, jnp.zeros_like(qh)), keys)
                     for g in range(ATT_GROUP)]

    def softmax_stage(h):
        probs[h] = []
        for g, s in enumerate(scores.pop(h)):
            s = jnp.where(valid, s, NEG)
            sink = sink_ref[h * ATT_GROUP + g]
            mx = jnp.maximum(jnp.max(s, axis=-1, keepdims=True), sink)
            p = jnp.exp(s - mx)
            den = jnp.sum(p, axis=-1, keepdims=True) + jnp.exp(sink - mx)
            probs[h].append((p.astype(BF16), 1.0 / den))

    def value_stage(h):
        sl = sls[h]
        vals = jnp.concatenate([vc_ref[0, :, sl], vp_ref[0, :, sl], v0_ref[0, :, sl], vn_ref[0, :, sl]], axis=0)
        acc = jnp.zeros((bq, gw), F32)
        for g, (p, inv) in enumerate(probs.pop(h)):
            acc = acc + jnp.where((lane // HEAD) == g, _dot(p, vals) * inv, 0.0)
        o_ref[0, :, sl] = acc.astype(BF16)

    for step in range(ATT_KV_HEADS + 2):
        if step < ATT_KV_HEADS:
            score_stage(step)
        if 1 <= step <= ATT_KV_HEADS:
            softmax_stage(step - 1)
        if step >= 2:
            value_stage(step - 2)


def _attention(q, k4, v4, sink, n_ctx):
    b, r, d = q.shape
    bq = WINDOW
    t = r - n_ctx
    nb = t // bq
    off = n_ctx // bq
    blk = lambda f: pl.BlockSpec((1, bq, d), f)
    ctx_spec = pl.BlockSpec((1, n_ctx, d), lambda bb, i: (bb, 0, 0))
    prev = lambda bb, i: (bb, jnp.maximum(i - 1, 0) + off, 0)
    cur = lambda bb, i: (bb, i + off, 0)
    nxt = lambda bb, i: (bb, jnp.minimum(i + 1, nb - 1) + off, 0)
    return pl.pallas_call(
        functools.partial(_attn_kernel, n_ctx),
        grid=(b, nb),
        in_specs=[pl.BlockSpec(memory_space=pltpu.SMEM), blk(cur), ctx_spec, ctx_spec,
                  blk(prev), blk(cur), blk(nxt), blk(prev), blk(cur), blk(nxt)],
        out_specs=pl.BlockSpec((1, bq, d), lambda bb, i: (bb, i, 0)),
        out_shape=jax.ShapeDtypeStruct((b, t, d), BF16),
        compiler_params=_cparams("parallel", "arbitrary"),
    )(sink, q, k4, v4, k4, k4, k4, v4, v4, v4)


def _rope_tables(n_ctx, t):
    half = HEAD // 2
    inv = 1.0 / (ROPE_BASE ** (jnp.arange(0, half, 2, dtype=F32) / half))
    pos = jnp.arange(t)
    ang_r = (pos // GRID_W).astype(F32)[:, None] * inv
    ang_c = (pos % GRID_W).astype(F32)[:, None] * inv
    cos = jnp.concatenate([jnp.cos(ang_r)] * 2 + [jnp.cos(ang_c)] * 2, axis=1)
    sin = jnp.concatenate([-jnp.sin(ang_r), jnp.sin(ang_r), -jnp.sin(ang_c), jnp.sin(ang_c)], axis=1)
    cos = jnp.concatenate([jnp.ones((n_ctx, HEAD), F32), cos], axis=0)
    sin = jnp.concatenate([jnp.zeros((n_ctx, HEAD), F32), sin], axis=0)
    return jnp.tile(cos, (1, 2)), jnp.tile(sin, (1, 2))


def kernel(x, c, ctx, c_ctx, ada_w, ada_b, ln1_g, ln1_b, ln2_g, ln2_b, rw_mu, rw_wr, rw_wk, rw_wv, rw_w0, rw_w1, rw_w2, rw_a0, rw_a1, rw_a2, rw_g1, rw_g2, rw_kk, rw_ka, rw_rk, rw_lnx_g, rw_lnx_b, rw_wo, at_wqkv, at_wo, at_sink, router_w, moe_w_gate, moe_w_up, moe_w_down):
    b, t, d = x.shape
    n_ctx = ctx.shape[1]
    depth = ada_w.shape[0]
    alpha = (2 * depth) ** 0.25
    assert n_ctx % ROW_TILE == 0 and t % ROW_TILE == 0 and b + 1 <= 8
    nctx_blk = n_ctx // ROW_TILE

    cc = jnp.concatenate([c, c_ctx[None], jnp.zeros((8 - b - 1, d), F32)], axis=0)
    mod_all = _ada_mod(cc, ada_w, ada_b).reshape(depth, 8, N_MOD, d)
    xa = None
    bf = lambda w: w.astype(BF16)
    pad_rows = lambda w, lo: jnp.pad(w, ((lo, LANES - lo - w.shape[0]), (0, 0)))

    for i in range(depth):
        last = i == depth - 1
        j = i // 2
        mods = jnp.stack([jnp.broadcast_to(mod_all[i, b], (b, N_MOD, d)), mod_all[i, :b]], axis=1).reshape(2 * b, N_MOD, d)
        lnv1 = jnp.stack([ln1_g[i], ln1_b[i], jnp.full((d,), alpha, F32)])
        lnv2 = jnp.stack([ln2_g[i], ln2_b[i], jnp.full((d,), alpha, F32)])
        rw_hi, rw_lo = _split(jnp.pad(router_w[i], ((0, 0), (0, LANES - N_EXPERTS))))
        rw_pad = jnp.concatenate([rw_hi, rw_hi, rw_lo], axis=0)
        if i % 2 == 0:
            vec = jnp.concatenate([rw_mu[j], rw_w0[j], rw_a0[j], rw_kk[j][None], rw_ka[j][None],
                                   rw_rk[j].reshape(1, d), jnp.zeros((3, d), F32)], axis=0)
            w1 = bf(jnp.concatenate([rw_w1[j, 0], rw_w1[j, 1]], axis=1))
            a1 = bf(jnp.concatenate([rw_a1[j, 0], rw_a1[j, 1]], axis=1))
            w2 = bf(jnp.stack([pad_rows(rw_w2[j, 0], 0), pad_rows(rw_w2[j, 1], HEAD)]))
            a2 = bf(jnp.stack([pad_rows(rw_a2[j, 0], 0), pad_rows(rw_a2[j, 1], HEAD)]))
            if xa is not None:
                ctx, x = xa[:, :n_ctx], xa[:, n_ctx:]
            (r, v, kk, g, bonus, lw0, k0, e0, lw1, k1, e1) = _rwkv_proj(
                ctx, x, mods, vec, bf(rw_wr[j]), bf(rw_wk[j]), bf(rw_wv[j]), w1, a1, bf(rw_g1[j]),
                w2, a2, bf(rw_g2[j]))
            yf, yr = _rwkv_scan(r, v, kk, lw0, k0, e0, lw1, k1, e1, n_ctx // CHUNK)
            lnx = jnp.stack([rw_lnx_g[j], rw_lnx_b[j]])
            x1, h2, afft = _rwkv_out(yf, yr, bonus, g, ctx, x, mods, lnx, bf(rw_wo[j]), lnv1, rw_pad)
        else:
            if xa is None:
                xa = jnp.concatenate([ctx, x], axis=1)
            nq = ATT_KV_HEADS * ATT_GROUP * HEAD
            nk = ATT_KV_HEADS * HEAD
            wqkv = at_wqkv[j]
            tile4 = lambda w: jnp.tile(w.reshape(d, ATT_KV_HEADS, 1, HEAD), (1, 1, ATT_GROUP, 1)).reshape(d, nq)
            cos, sin = _rope_tables(n_ctx, t)
            q, k4, v4 = _qkv(xa, mods, nctx_blk, cos, sin, bf(wqkv[:, :nq]),
                             bf(tile4(wqkv[:, nq:nq + nk])), bf(tile4(wqkv[:, nq + nk:])))
            o = _attention(q, k4, v4, at_sink[j], n_ctx)
            x1, h2, afft = _attn_out(o, xa, mods, nctx_blk, bf(at_wo[j]), lnv1, rw_pad)
        if last:
            if x1.shape[1] != t:
                x1, h2, afft = x1[:, n_ctx:], h2[:, n_ctx:], afft[:, :, n_ctx:]
            moe = _ec_moe(h2, afft, [(0, t)], i, moe_w_gate, moe_w_up, moe_w_down)
            return _moe_ln(x1, moe, mods, 0, lnv2)
        moe = _ec_moe(h2, afft, [(n_ctx, t), (0, n_ctx)], i, moe_w_gate, moe_w_up, moe_w_down)
        xa = _moe_ln(x1, moe, mods, nctx_blk, lnv2)
    return xa[:, n_ctx:]
```

```python
import functools

import jax
import jax.numpy as jnp
from jax import lax
from jax.experimental import pallas as pl
from jax.experimental.pallas import tpu as pltpu

F32 = jnp.float32
BF16 = jnp.bfloat16
I32 = jnp.int32
HIGHEST = lax.Precision.HIGHEST

LANES = 128
LN_EPS = 1e-5
N_MOD = 6
HEAD = 64
RW_LNX_EPS = 64e-5
RW_DECAY_SCALE = 0.606531
ATT_KV_HEADS = 4
ATT_GROUP = 4
WINDOW = 128
GRID_W = 64
ROPE_BASE = 10000.0
NEG = -1e30
N_EXPERTS = 16
EC_CAPACITY = 2
CHUNK = 64
DMA_UNROLL = 8
EXPERT_F_TILE = 256
EXPERT_SCATTER_STEPS = 2
EXPERT_GATHER_STEPS = 4
ROW_TILE = 256
VMEM_LIMIT = 56 * 2 ** 20


def _cparams(*sem):
    return pltpu.CompilerParams(dimension_semantics=sem, vmem_limit_bytes=VMEM_LIMIT)


def _dot(a, b):
    return jnp.dot(a, b, preferred_element_type=F32)


def _dot_nt(a, b):
    return lax.dot_general(a, b, (((1,), (1,)), ((), ())), preferred_element_type=F32)


def _split(x):
    hi = x.astype(BF16)
    lo = (x - hi.astype(F32)).astype(BF16)
    return hi, lo


def _head_sum(x):
    d = x.shape[-1]
    gr = (lax.broadcasted_iota(I32, (d, LANES), 0) // HEAD
          == lax.broadcasted_iota(I32, (d, LANES), 1)).astype(BF16)
    ge = (lax.broadcasted_iota(I32, (LANES, d), 0)
          == lax.broadcasted_iota(I32, (LANES, d), 1) // HEAD).astype(BF16)
    s = _dot(jnp.concatenate(_split(x), axis=1), jnp.concatenate([gr, gr], axis=0))
    return _dot(jnp.concatenate(_split(s), axis=1), jnp.concatenate([ge, ge], axis=0))


def _layer_norm(x, g, b):
    mu = jnp.mean(x, axis=-1, keepdims=True)
    xc = x - mu
    var = jnp.mean(xc * xc, axis=-1, keepdims=True)
    return xc * lax.rsqrt(var + LN_EPS) * g + b


def _ada_kernel(cc_ref, w_ref, b_ref, o_ref):
    cc = cc_ref[...]
    s = cc * jax.nn.sigmoid(cc)
    o_ref[0] = jnp.dot(s, w_ref[0], precision=HIGHEST, preferred_element_type=F32) + b_ref[0]


def _ada_mod(cc, ada_w, ada_b):
    depth, d, n = ada_w.shape
    tn = n // 4
    return pl.pallas_call(
        _ada_kernel,
        grid=(depth, n // tn),
        in_specs=[pl.BlockSpec((8, d), lambda l, j: (0, 0)),
                  pl.BlockSpec((1, d, tn), lambda l, j: (l, 0, j)),
                  pl.BlockSpec((1, 1, tn), lambda l, j: (l, 0, j))],
        out_specs=pl.BlockSpec((1, 8, tn), lambda l, j: (l, 0, j)),
        out_shape=jax.ShapeDtypeStruct((depth, 8, n), F32),
        compiler_params=_cparams("arbitrary", "arbitrary"),
    )(cc, ada_w, ada_b.reshape(depth, 1, n))


def _rwkv_proj_kernel(nctx_blk, c_ref, x_ref, xp_ref, xn_ref, mod_ref, vec_ref, wr_ref, wk_ref, wv_ref,
                      w1_ref, a1_ref, g1_ref, w2_ref, a2_ref, g2_ref,
                      r_ref, v_ref, kk_ref, g_ref, bonus_ref,
                      lw0_ref, k0_ref, e0_ref, lw1_ref, k1_ref, e1_ref):
    i = pl.program_id(1)
    nblk = pl.num_programs(1)
    tm = x_ref.shape[1]
    m = mod_ref[0]
    shift, scale = m[0:1], m[1:2]
    h = jnp.where(i < nctx_blk, c_ref[0], x_ref[0]) * (1 + scale) + shift
    first = (i == 0) | (i == nctx_blk)
    last = (i == nctx_blk - 1) | (i == nblk - 1)
    hp = jnp.where(first, 0.0, xp_ref[0][7:8] * (1 + scale) + shift)
    hn = jnp.where(last, 0.0, xn_ref[0][0:1] * (1 + scale) + shift)
    rows = lax.broadcasted_iota(I32, (tm, 1), 0)
    prev = jnp.where(rows == 0, hp, pltpu.roll(h, 1, 0))
    nxt = jnp.where(rows == tm - 1, hn, pltpu.roll(h, tm - 1, 0))
    xx = 0.5 * (prev + nxt) - h
    vec = vec_ref[...]
    mix = lambda j: (h + xx * vec[j:j + 1]).astype(BF16)
    r = _dot(mix(0), wr_ref[...])
    k = _dot(mix(2), wk_ref[...])
    v = _dot(mix(3), wv_ref[...])
    tw = jnp.tanh(_dot(mix(1), w1_ref[...])).astype(BF16)
    ta = _dot(mix(4), a1_ref[...]).astype(BF16)
    g = _dot(jax.nn.sigmoid(_dot(mix(5), g1_ref[...])).astype(BF16), g2_ref[...])
    kk = k * vec[10:11]
    nrm = jnp.sqrt(_head_sum(kk * kk))
    kk = kk / jnp.maximum(nrm, 1e-12)
    bonus = _head_sum(r * k * vec[12:13]) * v
    r_ref[0] = r
    v_ref[0] = v
    kk_ref[0] = kk
    g_ref[0] = g
    bonus_ref[0] = bonus
    for d, (lw_ref, kd_ref, e_ref) in enumerate(((lw0_ref, k0_ref, e0_ref), (lw1_ref, k1_ref, e1_ref))):
        lw_ref[0] = -RW_DECAY_SCALE * jax.nn.sigmoid(vec[6 + d:7 + d] + _dot(tw, w2_ref[d]))
        eta = jax.nn.sigmoid(vec[8 + d:9 + d] + _dot(ta, a2_ref[d]))
        e_ref[0] = eta
        kd_ref[0] = k * (1 + (eta - 1) * vec[11:12])


def _rwkv_proj(ctx, x, mods, vec, wr, wk, wv, w1, a1, g1, w2, a2, g2):
    b, t, d = x.shape
    tm = ROW_TILE
    assert ctx.shape[1] == tm
    nctx_blk = 1
    r = ctx.shape[1] + t
    nblk = r // tm
    hb = tm // 8
    row = pl.BlockSpec((1, tm, d), lambda bb, i: (bb, i, 0))
    lat = lambda i: jnp.maximum(i - nctx_blk, 0)
    full = lambda a: pl.BlockSpec(a.shape, lambda bb, i: (0,) * a.ndim)
    out = jax.ShapeDtypeStruct((b, r, d), F32)
    return pl.pallas_call(
        functools.partial(_rwkv_proj_kernel, nctx_blk),
        grid=(b, nblk),
        in_specs=[pl.BlockSpec((1, tm, d), lambda bb, i: (bb, 0, 0)),
                  pl.BlockSpec((1, tm, d), lambda bb, i: (bb, lat(i), 0)),
                  pl.BlockSpec((1, 8, d), lambda bb, i: (bb, jnp.maximum(lat(i) * hb - 1, 0), 0)),
                  pl.BlockSpec((1, 8, d), lambda bb, i: (bb, jnp.minimum((lat(i) + 1) * hb, t // 8 - 1), 0)),
                  pl.BlockSpec((1, N_MOD, d), lambda bb, i: (bb * 2 + (i >= nctx_blk).astype(I32), 0, 0)),
                  full(vec), full(wr), full(wk), full(wv), full(w1), full(a1), full(g1),
                  full(w2), full(a2), full(g2)],
        out_specs=[row] * 11,
        out_shape=[out] * 11,
        compiler_params=_cparams("parallel", "arbitrary"),
    )(ctx, x, x, x, mods, vec, wr, wk, wv, w1, a1, g1, w2, a2, g2)


def _scan_kernel(rf_ref, vf_ref, kkf_ref, lw0_ref, k0_ref, e0_ref,
                 rr_ref, vr_ref, kkr_ref, lw1_ref, k1_ref, e1_ref,
                 yf_ref, yr_ref, s_ref):
    c = CHUNK
    assert c == HEAD

    @pl.when(pl.program_id(0) == 0)
    def _():
        s_ref[...] = jnp.zeros_like(s_ref)

    ti = lax.broadcasted_iota(I32, (c, c), 0)
    tj = lax.broadcasted_iota(I32, (c, c), 1)
    tt = lax.broadcasted_iota(I32, (c, 2 * c), 0)
    ss = lax.broadcasted_iota(I32, (c, 2 * c), 1) % c
    m0 = lax.broadcasted_iota(I32, (1, 2 * HEAD), 1) < HEAD

    def stack(x):
        return jnp.concatenate([jnp.where(m0, x, 0.0), jnp.where(m0, 0.0, x)], axis=0)

    dirs = ((rf_ref, vf_ref, kkf_ref, lw0_ref, k0_ref, e0_ref, yf_ref),
            (rr_ref, vr_ref, kkr_ref, lw1_ref, k1_ref, e1_ref, yr_ref))
    n_pair = s_ref.shape[2]
    ch = []
    for n, (d, (r_ref, v_ref, kk_ref, lw_ref, kd_ref, e_ref, y_ref)) in (
            (n, dr) for n in range(s_ref.shape[0]) for dr in enumerate(dirs)):
        if d == 0:
            tri = (tj <= ti).astype(BF16)
            strict, incl = ss < tt, ss <= tt
        else:
            tri = (tj >= ti).astype(BF16)
            strict, incl = ss > tt, ss >= tt
        lw = lw_ref[n]
        lw_hi = lw.astype(BF16)
        lw_mid = (lw - lw_hi.astype(F32)).astype(BF16)
        lw_lo = (lw - lw_hi.astype(F32) - lw_mid.astype(F32)).astype(BF16)
        cs = _dot(jnp.concatenate([tri, tri, tri], axis=1), jnp.concatenate([lw_hi, lw_mid, lw_lo], axis=0))
        gam = jnp.exp(cs)
        gam_inv = jnp.exp(-cs)
        kk = kk_ref[n]
        a_t = -(kk * jnp.exp(cs - lw))
        b_t = kk * e_ref[n] * gam_inv
        k_t = kd_ref[n] * gam_inv
        r_t = r_ref[n] * gam
        v_all = v_ref[n]
        g_end = gam[c - 1:c] if d == 0 else gam[0:1]
        for p in range(n_pair):
            sl = slice(2 * HEAD * p, 2 * HEAD * (p + 1))
            ch.append(dict(
                at=(n, d, p), sl=sl, y_ref=y_ref, strict=strict, incl=incl, v2=stack(v_all[:, sl]),
                ar=jnp.concatenate([a_t[:, sl], r_t[:, sl]], axis=0).astype(BF16),
                bk=jnp.concatenate([stack(b_t[:, sl]), stack(k_t[:, sl])], axis=0),
                ge=g_end[:, sl], s0=s_ref[n, d, p]))
    for q in ch:
        q["m1"] = _dot_nt(q["ar"], q["bk"].astype(BF16))
    for q in ch:
        q["ars"] = _dot_nt(q["ar"], q["s0"].astype(BF16))
    for q in ch:
        m1 = q.pop("m1")
        q["pw"] = jnp.where(q["strict"], m1[:c, :2 * c], 0.0)
        lak = jnp.where(q["strict"], m1[:c, 2 * c:], 0.0).astype(BF16)
        q["lr"] = jnp.concatenate([jnp.where(q["incl"], m1[c:, :2 * c], 0.0),
                                   jnp.where(q["incl"], m1[c:, 2 * c:], 0.0)], axis=1).astype(BF16)
        q["x"] = q["ars"][:c] + _dot(lak, q["v2"].astype(BF16))
    for it in range(6):
        for q in ch:
            pb = q["pw"].astype(BF16)
            if it < 5:
                px = _dot(pb, jnp.concatenate([stack(q["x"]), stack(q["pw"])], axis=1).astype(BF16))
                q["x"] = q["x"] + px[:, :2 * c]
                q["pw"] = px[:, 2 * c:]
            else:
                q["x"] = q["x"] + _dot(pb, stack(q["x"]).astype(BF16))
    for q in ch:
        q["uv"] = jnp.concatenate([stack(q["x"]), q["v2"]], axis=0)
        q["y_ref"][q["at"][0], :, q["sl"]] = q["ars"][c:] + _dot(q["lr"], q["uv"].astype(BF16))
    for q in ch:
        s_ref[q["at"]] = (q["s0"] * q["ge"]
                          + _dot(q["uv"].T.astype(BF16), (q["bk"] * q["ge"]).astype(BF16)))


def _rwkv_scan(r, v, kk, lw0, k0, e0, lw1, k1, e1, nctx_chunks):
    b, rows, d = r.shape
    c = CHUNK
    nc = rows // c
    fwd = lambda i: (0, i, 0)
    rev = lambda i: (0, jnp.where(i < nctx_chunks, nctx_chunks - 1 - i, nc - 1 - (i - nctx_chunks)), 0)
    sf = pl.BlockSpec((b, c, d), fwd)
    sr = pl.BlockSpec((b, c, d), rev)
    out = jax.ShapeDtypeStruct((b, rows, d), F32)
    return pl.pallas_call(
        _scan_kernel,
        grid=(nc,),
        in_specs=[sf] * 6 + [sr] * 6,
        out_specs=[sf, sr],
        out_shape=[out, out],
        scratch_shapes=[pltpu.VMEM((b, 2, d // (2 * HEAD), 2 * HEAD, 2 * HEAD), F32)],
        compiler_params=_cparams("arbitrary"),
    )(r, v, kk, lw0, k0, e0, r, v, kk, lw1, k1, e1)


def _post_mixer(z, x_ref, mod_ref, stream, wo_ref, lnv, rw_ref, x1_ref, h2_ref, afft_ref):
    n, tm, d = x_ref.shape
    o = _dot(z.astype(BF16), wo_ref[...])
    x1s, h2s = [], []
    for s in range(n):
        m = mod_ref[2 * s + stream]
        x1 = _layer_norm(lnv[2:3] * x_ref[s] + m[2:3] * o[s * tm:(s + 1) * tm], lnv[0:1], lnv[1:2])
        x1s.append(x1)
        h2s.append(x1 * (1 + m[4:5]) + m[3:4])
    h_hi, h_lo = _split(jnp.concatenate(h2s, axis=0))
    logits = _dot(jnp.concatenate([h_hi, h_lo, h_hi], axis=1), rw_ref[...])
    lane = lax.broadcasted_iota(I32, logits.shape, 1)
    logits = jnp.where(lane < N_EXPERTS, logits, NEG)
    ex = jnp.exp(logits - jnp.max(logits, axis=-1, keepdims=True))
    aff = ex / jnp.sum(ex, axis=-1, keepdims=True)
    for s in range(n):
        aff_s = aff[s * tm:(s + 1) * tm]
        x1_ref[s] = x1s[s]
        h2_ref[s, :, :d] = h2s[s]
        h2_ref[s, :, d:] = aff_s
        afft_ref[s] = aff_s.T[:N_EXPERTS]


def _rwkv_out_kernel(nctx_blk, yf_ref, yr_ref, bonus_ref, g_ref, c_ref, x_ref, mod_ref, lnx_ref, wo_ref, lnv_ref,
                     rw_ref, x1_ref, h2_ref, afft_ref):
    n, tm, d = x_ref.shape
    rows = lambda ref: ref[...].reshape(n * tm, d)
    y = rows(yf_ref) + rows(yr_ref)
    mu = _head_sum(y) * (1.0 / HEAD)
    yc = y - mu
    var = _head_sum(yc * yc) * (1.0 / HEAD)
    lnx = lnx_ref[...]
    yn = yc * lax.rsqrt(var + RW_LNX_EPS) * lnx[0:1] + lnx[1:2]
    z = (yn + rows(bonus_ref)) * rows(g_ref)
    is_lat = pl.program_id(0) >= nctx_blk
    xs = jnp.where(is_lat, x_ref[...], c_ref[...])
    _post_mixer(z, xs, mod_ref, is_lat.astype(I32), wo_ref, lnv_ref[...], rw_ref, x1_ref, h2_ref, afft_ref)


def _post_mixer_outs(b, rows, d, tm):
    specs = [pl.BlockSpec((b, tm, d), lambda i: (0, i, 0)),
             pl.BlockSpec((b, tm, d + LANES), lambda i: (0, i, 0)),
             pl.BlockSpec((b, N_EXPERTS, tm), lambda i: (0, 0, i))]
    shapes = [jax.ShapeDtypeStruct((b, rows, d), F32), jax.ShapeDtypeStruct((b, rows, d + LANES), F32),
              jax.ShapeDtypeStruct((b, N_EXPERTS, rows), F32)]
    return specs, shapes


def _rwkv_out(yf, yr, bonus, g, ctx, x, mods, lnx, wo, lnv, rw_pad):
    b, r, d = yf.shape
    tm = ROW_TILE
    assert ctx.shape[1] == tm
    nctx_blk = 1
    row = pl.BlockSpec((b, tm, d), lambda i: (0, i, 0))
    full = lambda a: pl.BlockSpec(a.shape, lambda i: (0,) * a.ndim)
    out_specs, out_shape = _post_mixer_outs(b, r, d, tm)
    return pl.pallas_call(
        functools.partial(_rwkv_out_kernel, nctx_blk),
        grid=(r // tm,),
        in_specs=[row] * 4 + [pl.BlockSpec((b, tm, d), lambda i: (0, 0, 0)),
                              pl.BlockSpec((b, tm, d), lambda i: (0, jnp.maximum(i - nctx_blk, 0), 0)),
                              full(mods), full(lnx), full(wo), full(lnv), full(rw_pad)],
        out_specs=out_specs,
        out_shape=out_shape,
        compiler_params=_cparams("arbitrary"),
    )(yf, yr, bonus, g, ctx, x, mods, lnx, wo, lnv, rw_pad)


def _attn_out_kernel(o_ref, x_ref, mod_ref, wo_ref, lnv_ref, rw_ref, x1_ref, h2_ref, afft_ref):
    n, tm, d = x_ref.shape
    _post_mixer(o_ref[...].reshape(n * tm, d), x_ref, mod_ref, 1, wo_ref, lnv_ref[...], rw_ref,
                x1_ref, h2_ref, afft_ref)


def _attn_out(o, xa, mods, blk_off, wo, lnv, rw_pad):
    b, t, d = o.shape
    tm = ROW_TILE
    full = lambda a: pl.BlockSpec(a.shape, lambda i: (0,) * a.ndim)
    out_specs, out_shape = _post_mixer_outs(b, t, d, tm)
    return pl.pallas_call(
        _attn_out_kernel,
        grid=(t // tm,),
        in_specs=[pl.BlockSpec((b, tm, d), lambda i: (0, i, 0)),
                  pl.BlockSpec((b, tm, d), lambda i: (0, i + blk_off, 0)),
                  full(mods), full(wo), full(lnv), full(rw_pad)],
        out_specs=out_specs,
        out_shape=out_shape,
        compiler_params=_cparams("arbitrary"),
    )(o, xa, mods, wo, lnv, rw_pad)


def _select_kernel(cap, aff_ref, idx_ref, c_ref, bnd_v, bnd_s, sem):
    n_e, tn = aff_ref.shape[1], aff_ref.shape[2]

    def bit_step(j, bits):
        cand = bits | jnp.left_shift(jnp.int32(1), 30 - j)
        cnt = jnp.sum((aff_ref[0] >= lax.bitcast_convert_type(cand, F32)).astype(F32), axis=1, keepdims=True)
        return jnp.where(cnt >= cap, cand, bits)

    bits = lax.fori_loop(0, 31, bit_step, jnp.zeros((n_e, 1), I32))
    thr = lax.bitcast_convert_type(bits, F32)
    need = cap - jnp.sum((aff_ref[0] > thr).astype(F32), axis=1, keepdims=True)
    triu = (lax.broadcasted_iota(I32, (LANES, LANES), 0) <= lax.broadcasted_iota(I32, (LANES, LANES), 1)).astype(BF16)
    ns = min(cap, 256)
    tc = min(tn, 1024)
    lane = lax.broadcasted_iota(I32, (1, LANES), 1)
    tie_off = jnp.zeros((n_e, 1), F32)
    sel_off = jnp.zeros((n_e, 1), F32)
    bnd = jnp.zeros((n_e, LANES), F32)
    for j in range(tn // LANES):
        cols = slice(j * LANES, (j + 1) * LANES)
        a = aff_ref[0, :, cols]
        eq = (a == thr).astype(F32)
        tie_rank = _dot(eq.astype(BF16), triu) + tie_off - eq
        sel = ((a > thr) | ((eq > 0) & (tie_rank < need))).astype(F32)
        c_ref[:, cols] = _dot(sel.astype(BF16), triu) + sel_off
        tie_off = tie_off + jnp.sum(eq, axis=1, keepdims=True)
        sel_off = sel_off + jnp.sum(sel, axis=1, keepdims=True)
        if (j + 1) % (tc // LANES) == 0:
            bnd = jnp.where(lane == (j + 1) // (tc // LANES), sel_off, bnd)
    bnd_v[...] = bnd.astype(I32)
    to_smem = pltpu.make_async_copy(bnd_v, bnd_s, sem)
    to_smem.start()
    to_smem.wait()

    for ci in range(cap // ns):
        slot = (lax.broadcasted_iota(I32, (ns, 1), 0) + ci * ns).astype(F32)

        def per_expert(e, out):
            def chunk(j, part):
                def compare():
                    ce = c_ref[pl.ds(e, 1), pl.ds(pl.multiple_of(j * tc, tc), tc)]
                    acc = part
                    for k in range(tc // LANES):
                        acc = acc + (ce[:, k * LANES:(k + 1) * LANES] <= slot).astype(F32)
                    return acc

                before, upto = bnd_s[e, j], bnd_s[e, j + 1]
                return lax.cond(upto <= ci * ns, lambda: part + float(tc // LANES),
                                lambda: lax.cond(before > ci * ns + ns - 1, lambda: part, compare))

            part = lax.fori_loop(0, tn // tc, chunk, jnp.zeros((ns, LANES), F32))
            return jnp.where(lane == e, jnp.sum(part, axis=1, keepdims=True), out)

        out = lax.fori_loop(0, n_e, per_expert, jnp.zeros((ns, LANES), F32))
        idx_ref[0, ci * ns:(ci + 1) * ns, :] = out.astype(I32)


def _select(afft, cap):
    b, n_e, tn = afft.shape
    return pl.pallas_call(
        functools.partial(_select_kernel, cap),
        grid=(b,),
        in_specs=[pl.BlockSpec((1, n_e, tn), lambda bb: (bb, 0, 0))],
        out_specs=pl.BlockSpec((1, cap, LANES), lambda bb: (bb, 0, 0)),
        out_shape=jax.ShapeDtypeStruct((b, cap, LANES), I32),
        scratch_shapes=[pltpu.VMEM((n_e, tn), F32), pltpu.VMEM((n_e, LANES), I32), pltpu.SMEM((n_e, LANES), I32),
                        pltpu.SemaphoreType.DMA(())],
        compiler_params=_cparams("parallel"),
    )(afft)


def _expert_kernel(idx_ref, h_hbm, wg_ref, wu_ref, wd_ref, acc_in, acc_hbm,
                   stage, accst, xin, gate, yacc, sem_x, sem_a, sem_s):
    del acc_in
    e = pl.program_id(0)
    f = pl.program_id(1)
    n_e = pl.num_programs(0)
    nf, mb, dx = stage.shape
    m, d = xin.shape
    ns, ng = EXPERT_SCATTER_STEPS, EXPERT_GATHER_STEPS
    bs, bg = nf // ns, nf // ng
    base = e * m
    prev = base - m
    nxt = lax.rem(e + 1, n_e) * m

    def x_copy(first, blk, u):
        return pltpu.make_async_copy(h_hbm.at[pl.ds(idx_ref[first + blk * mb + u], 1)],
                                     stage.at[blk, pl.ds(u, 1)], sem_x)

    def a_copy(blk, u):
        return pltpu.make_async_copy(acc_hbm.at[pl.ds(idx_ref[base + blk * mb + u], 1)],
                                     accst.at[blk, pl.ds(u, 1)], sem_a)

    def s_copy(first, blk, u):
        return pltpu.make_async_copy(accst.at[blk, pl.ds(u, 1)],
                                     acc_hbm.at[pl.ds(idx_ref[first + blk * mb + u], 1)], sem_s)

    def looped(copy_of):
        for blk in range(nf):
            def body(i, carry):
                for u in range(DMA_UNROLL):
                    copy_of(blk, i * DMA_UNROLL + u).start()
                return carry

            lax.fori_loop(0, mb // DMA_UNROLL, body, 0)

    def wait_blocks(src_of, dst_of, sem):
        for blk in range(nf):
            pltpu.make_async_copy(src_of(blk), dst_of(blk), sem).wait()

    wait_x = lambda: wait_blocks(lambda blk: h_hbm.at[pl.ds(0, mb)], lambda blk: stage.at[blk], sem_x)
    wait_s = lambda: wait_blocks(lambda blk: accst.at[blk], lambda blk: acc_hbm.at[pl.ds(0, mb)], sem_s)

    @pl.when((e == 0) & (f == 0))
    def _():
        looped(lambda blk, u: x_copy(base, blk, u))
        wait_x()

    @pl.when(f == 0)
    def _():
        rows = stage[...]
        xin[...] = rows[:, :, :d].reshape(m, d).astype(BF16)
        lane = lax.broadcasted_iota(I32, (1, 1, LANES), 2)
        gate[...] = jnp.sum(jnp.where(lane == e, rows[:, :, d:], 0.0), axis=2, keepdims=True).reshape(m, 1)
        yacc[...] = jnp.zeros_like(yacc)

    @pl.when((f == ns) & (e > 0))
    def _():
        wait_s()

    def step(extra_copies):
        for u in range(mb):
            x_copy(nxt, f, u).start()
        extra_copies()
        x = xin[...]
        a = _dot(x, wg_ref[0, 0].astype(BF16))
        up = _dot(x, wu_ref[0, 0].astype(BF16))
        hm = (a * jax.nn.sigmoid(a) * up).astype(BF16)
        wd = wd_ref[0, 0].astype(BF16)
        rb = m // 4
        for r0 in range(0, m, rb):
            yacc[r0:r0 + rb] += _dot(hm[r0:r0 + rb], wd)

    def scatter_copies():
        for k in range(bs):
            for u in range(mb):
                s_copy(prev, f * bs + k, u).start()

    def gather_copies():
        for k in range(bg):
            for u in range(mb):
                a_copy((f - ns) * bg + k, u).start()

    scatters = (f < ns) & (e > 0)
    gathers = (f >= ns) & (f < ns + ng)
    pl.when(scatters)(lambda: step(scatter_copies))
    pl.when(gathers)(lambda: step(gather_copies))
    pl.when(jnp.logical_not(scatters | gathers))(lambda: step(lambda: None))

    @pl.when(f == nf - 1)
    def _():
        wait_blocks(lambda blk: acc_hbm.at[pl.ds(0, mb)], lambda blk: accst.at[blk], sem_a)
        accst[...] = accst[...] + (gate[...] * yacc[...]).reshape(nf, mb, d)
        wait_x()

    @pl.when((f == nf - 1) & (e == n_e - 1))
    def _():
        looped(lambda blk, u: s_copy(base, blk, u))
        wait_s()


def _experts(idx_flat, m, h_rows, acc, layer, w_gate, w_up, w_down):
    _, n_e, d, f = w_gate.shape
    tf = EXPERT_F_TILE
    nf = f // tf
    mb = m // nf
    assert mb * nf == m and mb % DMA_UNROLL == 0 and m % 64 == 0 and h_rows.shape[1] == d + LANES
    assert nf % EXPERT_SCATTER_STEPS == 0 and nf % EXPERT_GATHER_STEPS == 0
    assert EXPERT_SCATTER_STEPS + EXPERT_GATHER_STEPS <= nf
    grid_spec = pltpu.PrefetchScalarGridSpec(
        num_scalar_prefetch=1,
        grid=(n_e, f // tf),
        in_specs=[pl.BlockSpec(memory_space=pl.ANY),
                  pl.BlockSpec((1, 1, d, tf), lambda e, j, idx: (layer, e, 0, j)),
                  pl.BlockSpec((1, 1, d, tf), lambda e, j, idx: (layer, e, 0, j)),
                  pl.BlockSpec((1, 1, tf, d), lambda e, j, idx: (layer, e, j, 0)),
                  pl.BlockSpec(memory_space=pl.ANY)],
        out_specs=pl.BlockSpec(memory_space=pl.ANY),
        scratch_shapes=[pltpu.VMEM((nf, mb, d + LANES), F32), pltpu.VMEM((nf, mb, d), F32),
                        pltpu.VMEM((m, d), BF16), pltpu.VMEM((m, 1), F32), pltpu.VMEM((m, d), F32),
                        pltpu.SemaphoreType.DMA(()), pltpu.SemaphoreType.DMA(()), pltpu.SemaphoreType.DMA(())],
    )
    return pl.pallas_call(
        _expert_kernel,
        grid_spec=grid_spec,
        out_shape=jax.ShapeDtypeStruct(acc.shape, F32),
        input_output_aliases={5: 0},
        compiler_params=_cparams("arbitrary", "arbitrary"),
    )(idx_flat, h_rows, w_gate, w_up, w_down, acc)


def _moe_ln_kernel(x_ref, moe_ref, mod_ref, lnv_ref, o_ref):
    lnv = lnv_ref[...]
    m = mod_ref[0]
    o_ref[0] = _layer_norm(lnv[2:3] * x_ref[0] + m[5:6] * moe_ref[0], lnv[0:1], lnv[1:2])


def _moe_ln(x1, moe, mods, nctx_blk, lnv):
    b, r, d = x1.shape
    tm = ROW_TILE
    row = pl.BlockSpec((1, tm, d), lambda bb, i: (bb, i, 0))
    return pl.pallas_call(
        _moe_ln_kernel,
        grid=(b, r // tm),
        in_specs=[row, row,
                  pl.BlockSpec((1, N_MOD, d), lambda bb, i: (bb * 2 + (i >= nctx_blk).astype(I32), 0, 0)),
                  pl.BlockSpec(lnv.shape, lambda bb, i: (0, 0))],
        out_specs=row,
        out_shape=jax.ShapeDtypeStruct((b, r, d), F32),
        compiler_params=_cparams("parallel", "arbitrary"),
    )(x1, moe, mods, lnv)


def _ec_moe(h2x, afft, streams, layer, w_gate, w_up, w_down):
    b, r, dx = h2x.shape
    d = dx - LANES
    idx_parts = []
    for off, tn in streams:
        cap = EC_CAPACITY * tn // N_EXPERTS
        idx = _select(lax.slice_in_dim(afft, off, off + tn, axis=2), cap)[:, :, :N_EXPERTS]
        rows = idx + (jnp.arange(b, dtype=I32) * r + off)[:, None, None]
        idx_parts.append(jnp.transpose(rows, (2, 0, 1)).reshape(N_EXPERTS, b * cap))
    idx_flat = jnp.concatenate(idx_parts, axis=1)
    acc = _experts(idx_flat.reshape(-1), idx_flat.shape[1], h2x.reshape(b * r, dx),
                   jnp.zeros((b * r, d), F32), layer, w_gate, w_up, w_down)
    return acc.reshape(b, r, d)


def _qkv_kernel(x_ref, mod_ref, cos_ref, sin_ref, wq_ref, wk_ref, wv_ref, q_ref, k_ref, v_ref):
    m = mod_ref[0]
    h = (x_ref[0] * (1 + m[1:2]) + m[0:1]).astype(BF16)
    d = x_ref.shape[2]
    rep = d // LANES
    cos = jnp.concatenate([cos_ref[...]] * rep, axis=1)
    sin = jnp.concatenate([sin_ref[...]] * rep, axis=1)
    lane = lax.broadcasted_iota(I32, (1, d), 1)
    lower = (lane % 32) < 16

    def rope(t):
        partner = jnp.where(lower, pltpu.roll(t, d - 16, 1), pltpu.roll(t, 16, 1))
        return t * cos + partner * sin

    q_ref[0] = (rope(_dot(h, wq_ref[...])) * (HEAD ** -0.5)).astype(BF16)
    k_ref[0] = rope(_dot(h, wk_ref[...])).astype(BF16)
    v_ref[0] = _dot(h, wv_ref[...]).astype(BF16)


def _qkv(xa, mods, nctx_blk, cos, sin, wq, wk4, wv4):
    b, r, d = xa.shape
    tm = ROW_TILE
    row = pl.BlockSpec((1, tm, d), lambda bb, i: (bb, i, 0))
    tab = pl.BlockSpec((tm, LANES), lambda bb, i: (i, 0))
    full = lambda a: pl.BlockSpec(a.shape, lambda bb, i: (0,) * a.ndim)
    out = jax.ShapeDtypeStruct((b, r, d), BF16)
    return pl.pallas_call(
        _qkv_kernel,
        grid=(b, r // tm),
        in_specs=[row, pl.BlockSpec((1, N_MOD, d), lambda bb, i: (bb * 2 + (i >= nctx_blk).astype(I32), 0, 0)),
                  tab, tab, full(wq), full(wk4), full(wv4)],
        out_specs=[row] * 3,
        out_shape=[out] * 3,
        compiler_params=_cparams("parallel", "arbitrary"),
    )(xa, mods, cos, sin, wq, wk4, wv4)


def _attn_kernel(n_ctx, sink_ref, q_ref, kc_ref, vc_ref, kp_ref, k0_ref, kn_ref, vp_ref, v0_ref, vn_ref, o_ref):
    nb = pl.program_id(1)
    n_blk = pl.num_programs(1)
    bq = q_ref.shape[1]
    gw = ATT_GROUP * HEAD
    nk = n_ctx + 3 * bq
    col = lax.broadcasted_iota(I32, (bq, nk), 1)
    qpos = nb * bq + lax.broadcasted_iota(I32, (bq, nk), 0)
    kpos = (nb - 1) * bq + (col - n_ctx)
    valid = (col < n_ctx) | ((kpos >= 0) & (kpos < n_blk * bq) & (jnp.abs(qpos - kpos) <= WINDOW))
    lane = lax.broadcasted_iota(I32, (1, gw), 1)
    sls = [slice(gw * h, gw * (h + 1)) for h in range(ATT_KV_HEADS)]
    scores, probs = {}, {}

    def score_stage(h):
        sl = sls[h]
        keys = jnp.concatenate([kc_ref[0, :, sl], kp_ref[0, :, sl], k0_ref[0, :, sl], kn_ref[0, :, sl]], axis=0)
        qh = q_ref[0, :, sl]
        scores[h] = [_dot_nt(jnp.where((lane // HEAD) == g, qh, jnp.zeros_like(qh)), keys)
                     for g in range(ATT_GROUP)]

    def softmax_stage(h):
        probs[h] = []
        for g, s in enumerate(scores.pop(h)):
            s = jnp.where(valid, s, NEG)
            sink = sink_ref[h * ATT_GROUP + g]
            mx = jnp.maximum(jnp.max(s, axis=-1, keepdims=True), sink)
            p = jnp.exp(s - mx)
            den = jnp.sum(p, axis=-1, keepdims=True) + jnp.exp(sink - mx)
            probs[h].append((p.astype(BF16), 1.0 / den))

    def value_stage(h):
        sl = sls[h]
        sl2 = slice(gw * h, gw * h + 2 * HEAD)
        vals = jnp.concatenate([vc_ref[0, :, sl2], vp_ref[0, :, sl2], v0_ref[0, :, sl2], vn_ref[0, :, sl2]], axis=0)
        low = lane[:, :2 * HEAD] < HEAD
        zero = jnp.zeros_like(vals)
        vals2 = jnp.concatenate([jnp.where(low, vals, zero), jnp.where(low, zero, vals)], axis=0)
        pg = probs.pop(h)
        for g in range(0, ATT_GROUP, 2):
            (p0, inv0), (p1, inv1) = pg[g], pg[g + 1]
            o2 = _dot(jnp.concatenate([p0, p1], axis=1), vals2) * jnp.where(low, inv0, inv1)
            o_ref[0, :, gw * h + g * HEAD:gw * h + (g + 2) * HEAD] = o2.astype(BF16)

    for step in range(ATT_KV_HEADS + 2):
        if step < ATT_KV_HEADS:
            score_stage(step)
        if 1 <= step <= ATT_KV_HEADS:
            softmax_stage(step - 1)
        if step >= 2:
            value_stage(step - 2)


def _attention(q, k4, v4, sink, n_ctx):
    b, r, d = q.shape
    bq = WINDOW
    t = r - n_ctx
    nb = t // bq
    off = n_ctx // bq
    blk = lambda f: pl.BlockSpec((1, bq, d), f)
    ctx_spec = pl.BlockSpec((1, n_ctx, d), lambda bb, i: (bb, 0, 0))
    prev = lambda bb, i: (bb, jnp.maximum(i - 1, 0) + off, 0)
    cur = lambda bb, i: (bb, i + off, 0)
    nxt = lambda bb, i: (bb, jnp.minimum(i + 1, nb - 1) + off, 0)
    return pl.pallas_call(
        functools.partial(_attn_kernel, n_ctx),
        grid=(b, nb),
        in_specs=[pl.BlockSpec(memory_space=pltpu.SMEM), blk(cur), ctx_spec, ctx_spec,
                  blk(prev), blk(cur), blk(nxt), blk(prev), blk(cur), blk(nxt)],
        out_specs=pl.BlockSpec((1, bq, d), lambda bb, i: (bb, i, 0)),
        out_shape=jax.ShapeDtypeStruct((b, t, d), BF16),
        compiler_params=_cparams("parallel", "arbitrary"),
    )(sink, q, k4, v4, k4, k4, k4, v4, v4, v4)


def _rope_tables(n_ctx, t):
    half = HEAD // 2
    inv = 1.0 / (ROPE_BASE ** (jnp.arange(0, half, 2, dtype=F32) / half))
    pos = jnp.arange(t)
    ang_r = (pos // GRID_W).astype(F32)[:, None] * inv
    ang_c = (pos % GRID_W).astype(F32)[:, None] * inv
    cos = jnp.concatenate([jnp.cos(ang_r)] * 2 + [jnp.cos(ang_c)] * 2, axis=1)
    sin = jnp.concatenate([-jnp.sin(ang_r), jnp.sin(ang_r), -jnp.sin(ang_c), jnp.sin(ang_c)], axis=1)
    cos = jnp.concatenate([jnp.ones((n_ctx, HEAD), F32), cos], axis=0)
    sin = jnp.concatenate([jnp.zeros((n_ctx, HEAD), F32), sin], axis=0)
    return jnp.tile(cos, (1, 2)), jnp.tile(sin, (1, 2))


def kernel(x, c, ctx, c_ctx, ada_w, ada_b, ln1_g, ln1_b, ln2_g, ln2_b, rw_mu, rw_wr, rw_wk, rw_wv, rw_w0, rw_w1, rw_w2, rw_a0, rw_a1, rw_a2, rw_g1, rw_g2, rw_kk, rw_ka, rw_rk, rw_lnx_g, rw_lnx_b, rw_wo, at_wqkv, at_wo, at_sink, router_w, moe_w_gate, moe_w_up, moe_w_down):
    b, t, d = x.shape
    n_ctx = ctx.shape[1]
    depth = ada_w.shape[0]
    alpha = (2 * depth) ** 0.25
    assert n_ctx % ROW_TILE == 0 and t % ROW_TILE == 0 and b + 1 <= 8
    nctx_blk = n_ctx // ROW_TILE

    cc = jnp.concatenate([c, c_ctx[None], jnp.zeros((8 - b - 1, d), F32)], axis=0)
    mod_all = _ada_mod(cc, ada_w, ada_b).reshape(depth, 8, N_MOD, d)
    xa = None
    bf = lambda w: w.astype(BF16)
    pad_rows = lambda w, lo: jnp.pad(w, ((lo, LANES - lo - w.shape[0]), (0, 0)))

    for i in range(depth):
        last = i == depth - 1
        j = i // 2
        mods = jnp.stack([jnp.broadcast_to(mod_all[i, b], (b, N_MOD, d)), mod_all[i, :b]], axis=1).reshape(2 * b, N_MOD, d)
        lnv1 = jnp.stack([ln1_g[i], ln1_b[i], jnp.full((d,), alpha, F32)])
        lnv2 = jnp.stack([ln2_g[i], ln2_b[i], jnp.full((d,), alpha, F32)])
        rw_hi, rw_lo = _split(jnp.pad(router_w[i], ((0, 0), (0, LANES - N_EXPERTS))))
        rw_pad = jnp.concatenate([rw_hi, rw_hi, rw_lo], axis=0)
        if i % 2 == 0:
            vec = jnp.concatenate([rw_mu[j], rw_w0[j], rw_a0[j], rw_kk[j][None], rw_ka[j][None],
                                   rw_rk[j].reshape(1, d), jnp.zeros((3, d), F32)], axis=0)
            w1 = bf(jnp.concatenate([rw_w1[j, 0], rw_w1[j, 1]], axis=1))
            a1 = bf(jnp.concatenate([rw_a1[j, 0], rw_a1[j, 1]], axis=1))
            w2 = bf(jnp.stack([pad_rows(rw_w2[j, 0], 0), pad_rows(rw_w2[j, 1], HEAD)]))
            a2 = bf(jnp.stack([pad_rows(rw_a2[j, 0], 0), pad_rows(rw_a2[j, 1], HEAD)]))
            if xa is not None:
                ctx, x = xa[:, :n_ctx], xa[:, n_ctx:]
            (r, v, kk, g, bonus, lw0, k0, e0, lw1, k1, e1) = _rwkv_proj(
                ctx, x, mods, vec, bf(rw_wr[j]), bf(rw_wk[j]), bf(rw_wv[j]), w1, a1, bf(rw_g1[j]),
                w2, a2, bf(rw_g2[j]))
            yf, yr = _rwkv_scan(r, v, kk, lw0, k0, e0, lw1, k1, e1, n_ctx // CHUNK)
            lnx = jnp.stack([rw_lnx_g[j], rw_lnx_b[j]])
            x1, h2, afft = _rwkv_out(yf, yr, bonus, g, ctx, x, mods, lnx, bf(rw_wo[j]), lnv1, rw_pad)
        else:
            if xa is None:
                xa = jnp.concatenate([ctx, x], axis=1)
            nq = ATT_KV_HEADS * ATT_GROUP * HEAD
            nk = ATT_KV_HEADS * HEAD
            wqkv = at_wqkv[j]
            tile4 = lambda w: jnp.tile(w.reshape(d, ATT_KV_HEADS, 1, HEAD), (1, 1, ATT_GROUP, 1)).reshape(d, nq)
            cos, sin = _rope_tables(n_ctx, t)
            q, k4, v4 = _qkv(xa, mods, nctx_blk, cos, sin, bf(wqkv[:, :nq]),
                             bf(tile4(wqkv[:, nq:nq + nk])), bf(tile4(wqkv[:, nq + nk:])))
            o = _attention(q, k4, v4, at_sink[j], n_ctx)
            x1, h2, afft = _attn_out(o, xa, mods, nctx_blk, bf(at_wo[j]), lnv1, rw_pad)
        if last:
            if x1.shape[1] != t:
                x1, h2, afft = x1[:, n_ctx:], h2[:, n_ctx:], afft[:, :, n_ctx:]
            moe = _ec_moe(h2, afft, [(0, t)], i, moe_w_gate, moe_w_up, moe_w_down)
            return _moe_ln(x1, moe, mods, 0, lnv2)
        moe = _ec_moe(h2, afft, [(n_ctx, t), (0, n_ctx)], i, moe_w_gate, moe_w_up, moe_w_down)
        xa = _moe_ln(x1, moe, mods, nctx_blk, lnv2)
    return xa[:, n_ctx:]
```

```python
import functools

import jax
import jax.numpy as jnp
from jax import lax
from jax.experimental import pallas as pl
from jax.experimental.pallas import tpu as pltpu

F32 = jnp.float32
BF16 = jnp.bfloat16
I32 = jnp.int32
HIGHEST = lax.Precision.HIGHEST

LANES = 128
LN_EPS = 1e-5
N_MOD = 6
HEAD = 64
RW_LNX_EPS = 64e-5
RW_DECAY_SCALE = 0.606531
ATT_KV_HEADS = 4
ATT_GROUP = 4
WINDOW = 128
GRID_W = 64
ROPE_BASE = 10000.0
NEG = -1e30
N_EXPERTS = 16
EC_CAPACITY = 2
CHUNK = 64
DMA_UNROLL = 8
EXPERT_F_TILE = 256
EXPERT_SCATTER_STEPS = 2
EXPERT_GATHER_STEPS = 4
ROW_TILE = 256
VMEM_LIMIT = 56 * 2 ** 20


def _cparams(*sem):
    return pltpu.CompilerParams(dimension_semantics=sem, vmem_limit_bytes=VMEM_LIMIT)


def _dot(a, b):
    return jnp.dot(a, b, preferred_element_type=F32)


def _dot_nt(a, b):
    return lax.dot_general(a, b, (((1,), (1,)), ((), ())), preferred_element_type=F32)


def _split(x):
    hi = x.astype(BF16)
    lo = (x - hi.astype(F32)).astype(BF16)
    return hi, lo


def _head_sum(x):
    d = x.shape[-1]
    gr = (lax.broadcasted_iota(I32, (d, LANES), 0) // HEAD
          == lax.broadcasted_iota(I32, (d, LANES), 1)).astype(BF16)
    ge = (lax.broadcasted_iota(I32, (LANES, d), 0)
          == lax.broadcasted_iota(I32, (LANES, d), 1) // HEAD).astype(BF16)
    s = _dot(jnp.concatenate(_split(x), axis=1), jnp.concatenate([gr, gr], axis=0))
    return _dot(jnp.concatenate(_split(s), axis=1), jnp.concatenate([ge, ge], axis=0))


def _layer_norm(x, g, b):
    mu = jnp.mean(x, axis=-1, keepdims=True)
    xc = x - mu
    var = jnp.mean(xc * xc, axis=-1, keepdims=True)
    return xc * lax.rsqrt(var + LN_EPS) * g + b


def _ada_kernel(cc_ref, w_ref, b_ref, o_ref):
    cc = cc_ref[...]
    s = cc * jax.nn.sigmoid(cc)
    o_ref[0] = jnp.dot(s, w_ref[0], precision=HIGHEST, preferred_element_type=F32) + b_ref[0]


def _ada_mod(cc, ada_w, ada_b):
    depth, d, n = ada_w.shape
    tn = n // 4
    return pl.pallas_call(
        _ada_kernel,
        grid=(depth, n // tn),
        in_specs=[pl.BlockSpec((8, d), lambda l, j: (0, 0)),
                  pl.BlockSpec((1, d, tn), lambda l, j: (l, 0, j)),
                  pl.BlockSpec((1, 1, tn), lambda l, j: (l, 0, j))],
        out_specs=pl.BlockSpec((1, 8, tn), lambda l, j: (l, 0, j)),
        out_shape=jax.ShapeDtypeStruct((depth, 8, n), F32),
        compiler_params=_cparams("arbitrary", "arbitrary"),
    )(cc, ada_w, ada_b.reshape(depth, 1, n))


def _rwkv_proj_kernel(nctx_blk, c_ref, x_ref, xp_ref, xn_ref, mod_ref, vec_ref, wr_ref, wk_ref, wv_ref,
                      w1_ref, a1_ref, g1_ref, w2_ref, a2_ref, g2_ref,
                      r_ref, v_ref, kk_ref, g_ref, bonus_ref,
                      lw0_ref, k0_ref, e0_ref, lw1_ref, k1_ref, e1_ref):
    i = pl.program_id(1)
    nblk = pl.num_programs(1)
    tm = x_ref.shape[1]
    m = mod_ref[0]
    shift, scale = m[0:1], m[1:2]
    h = jnp.where(i < nctx_blk, c_ref[0], x_ref[0]) * (1 + scale) + shift
    first = (i == 0) | (i == nctx_blk)
    last = (i == nctx_blk - 1) | (i == nblk - 1)
    hp = jnp.where(first, 0.0, xp_ref[0][7:8] * (1 + scale) + shift)
    hn = jnp.where(last, 0.0, xn_ref[0][0:1] * (1 + scale) + shift)
    rows = lax.broadcasted_iota(I32, (tm, 1), 0)
    prev = jnp.where(rows == 0, hp, pltpu.roll(h, 1, 0))
    nxt = jnp.where(rows == tm - 1, hn, pltpu.roll(h, tm - 1, 0))
    xx = 0.5 * (prev + nxt) - h
    vec = vec_ref[...]
    mix = lambda j: (h + xx * vec[j:j + 1]).astype(BF16)
    r = _dot(mix(0), wr_ref[...])
    k = _dot(mix(2), wk_ref[...])
    v = _dot(mix(3), wv_ref[...])
    tw = jnp.tanh(_dot(mix(1), w1_ref[...])).astype(BF16)
    ta = _dot(mix(4), a1_ref[...]).astype(BF16)
    g = _dot(jax.nn.sigmoid(_dot(mix(5), g1_ref[...])).astype(BF16), g2_ref[...])
    kk = k * vec[10:11]
    nrm = jnp.sqrt(_head_sum(kk * kk))
    kk = kk / jnp.maximum(nrm, 1e-12)
    bonus = _head_sum(r * k * vec[12:13]) * v
    r_ref[0] = r
    v_ref[0] = v
    kk_ref[0] = kk
    g_ref[0] = g
    bonus_ref[0] = bonus
    for d, (lw_ref, kd_ref, e_ref) in enumerate(((lw0_ref, k0_ref, e0_ref), (lw1_ref, k1_ref, e1_ref))):
        lw_ref[0] = -RW_DECAY_SCALE * jax.nn.sigmoid(vec[6 + d:7 + d] + _dot(tw, w2_ref[d]))
        eta = jax.nn.sigmoid(vec[8 + d:9 + d] + _dot(ta, a2_ref[d]))
        e_ref[0] = eta
        kd_ref[0] = k * (1 + (eta - 1) * vec[11:12])


def _rwkv_proj(ctx, x, mods, vec, wr, wk, wv, w1, a1, g1, w2, a2, g2):
    b, t, d = x.shape
    tm = ROW_TILE
    assert ctx.shape[1] == tm
    nctx_blk = 1
    r = ctx.shape[1] + t
    nblk = r // tm
    hb = tm // 8
    row = pl.BlockSpec((1, tm, d), lambda bb, i: (bb, i, 0))
    lat = lambda i: jnp.maximum(i - nctx_blk, 0)
    full = lambda a: pl.BlockSpec(a.shape, lambda bb, i: (0,) * a.ndim)
    out = jax.ShapeDtypeStruct((b, r, d), F32)
    return pl.pallas_call(
        functools.partial(_rwkv_proj_kernel, nctx_blk),
        grid=(b, nblk),
        in_specs=[pl.BlockSpec((1, tm, d), lambda bb, i: (bb, 0, 0)),
                  pl.BlockSpec((1, tm, d), lambda bb, i: (bb, lat(i), 0)),
                  pl.BlockSpec((1, 8, d), lambda bb, i: (bb, jnp.maximum(lat(i) * hb - 1, 0), 0)),
                  pl.BlockSpec((1, 8, d), lambda bb, i: (bb, jnp.minimum((lat(i) + 1) * hb, t // 8 - 1), 0)),
                  pl.BlockSpec((1, N_MOD, d), lambda bb, i: (bb * 2 + (i >= nctx_blk).astype(I32), 0, 0)),
                  full(vec), full(wr), full(wk), full(wv), full(w1), full(a1), full(g1),
                  full(w2), full(a2), full(g2)],
        out_specs=[row] * 11,
        out_shape=[out] * 11,
        compiler_params=_cparams("parallel", "arbitrary"),
    )(ctx, x, x, x, mods, vec, wr, wk, wv, w1, a1, g1, w2, a2, g2)


def _scan_kernel(rf_ref, vf_ref, kkf_ref, lw0_ref, k0_ref, e0_ref,
                 rr_ref, vr_ref, kkr_ref, lw1_ref, k1_ref, e1_ref,
                 yf_ref, yr_ref, s_ref):
    c = CHUNK
    assert c == HEAD

    @pl.when(pl.program_id(0) == 0)
    def _():
        s_ref[...] = jnp.zeros_like(s_ref)

    ti = lax.broadcasted_iota(I32, (c, c), 0)
    tj = lax.broadcasted_iota(I32, (c, c), 1)
    tt = lax.broadcasted_iota(I32, (c, 2 * c), 0)
    ss = lax.broadcasted_iota(I32, (c, 2 * c), 1) % c
    m0 = lax.broadcasted_iota(I32, (1, 2 * HEAD), 1) < HEAD

    def stack(x):
        return jnp.concatenate([jnp.where(m0, x, 0.0), jnp.where(m0, 0.0, x)], axis=0)

    dirs = ((rf_ref, vf_ref, kkf_ref, lw0_ref, k0_ref, e0_ref, yf_ref),
            (rr_ref, vr_ref, kkr_ref, lw1_ref, k1_ref, e1_ref, yr_ref))
    n_pair = s_ref.shape[2]
    ch = []
    for n, (d, (r_ref, v_ref, kk_ref, lw_ref, kd_ref, e_ref, y_ref)) in (
            (n, dr) for n in range(s_ref.shape[0]) for dr in enumerate(dirs)):
        if d == 0:
            tri = (tj <= ti).astype(BF16)
            strict, incl = ss < tt, ss <= tt
        else:
            tri = (tj >= ti).astype(BF16)
            strict, incl = ss > tt, ss >= tt
        lw = lw_ref[n]
        lw_hi = lw.astype(BF16)
        lw_mid = (lw - lw_hi.astype(F32)).astype(BF16)
        lw_lo = (lw - lw_hi.astype(F32) - lw_mid.astype(F32)).astype(BF16)
        cs = _dot(jnp.concatenate([tri, tri, tri], axis=1), jnp.concatenate([lw_hi, lw_mid, lw_lo], axis=0))
        gam = jnp.exp(cs)
        gam_inv = jnp.exp(-cs)
        kk = kk_ref[n]
        a_t = -(kk * jnp.exp(cs - lw))
        b_t = kk * e_ref[n] * gam_inv
        k_t = kd_ref[n] * gam_inv
        r_t = r_ref[n] * gam
        v_all = v_ref[n]
        g_end = gam[c - 1:c] if d == 0 else gam[0:1]
        for p in range(n_pair):
            sl = slice(2 * HEAD * p, 2 * HEAD * (p + 1))
            ch.append(dict(
                at=(n, d, p), sl=sl, y_ref=y_ref, strict=strict, incl=incl, v2=stack(v_all[:, sl]),
                ar=jnp.concatenate([a_t[:, sl], r_t[:, sl]], axis=0).astype(BF16),
                bk=jnp.concatenate([stack(b_t[:, sl]), stack(k_t[:, sl])], axis=0),
                ge=g_end[:, sl], s0=s_ref[n, d, p]))
    for q in ch:
        q["m1"] = _dot_nt(q["ar"], q["bk"].astype(BF16))
    for q in ch:
        q["ars"] = _dot_nt(q["ar"], q["s0"].astype(BF16))
    for q in ch:
        m1 = q.pop("m1")
        q["pw"] = jnp.where(q["strict"], m1[:c, :2 * c], 0.0)
        lak = jnp.where(q["strict"], m1[:c, 2 * c:], 0.0).astype(BF16)
        q["lr"] = jnp.concatenate([jnp.where(q["incl"], m1[c:, :2 * c], 0.0),
                                   jnp.where(q["incl"], m1[c:, 2 * c:], 0.0)], axis=1).astype(BF16)
        q["x"] = q["ars"][:c] + _dot(lak, q["v2"].astype(BF16))
    for it in range(6):
        for q in ch:
            pb = q["pw"].astype(BF16)
            if it < 5:
                px = _dot(pb, jnp.concatenate([stack(q["x"]), stack(q["pw"])], axis=1).astype(BF16))
                q["x"] = q["x"] + px[:, :2 * c]
                q["pw"] = px[:, 2 * c:]
            else:
                q["x"] = q["x"] + _dot(pb, stack(q["x"]).astype(BF16))
    for q in ch:
        q["uv"] = jnp.concatenate([stack(q["x"]), q["v2"]], axis=0)
        q["y_ref"][q["at"][0], :, q["sl"]] = q["ars"][c:] + _dot(q["lr"], q["uv"].astype(BF16))
    for q in ch:
        s_ref[q["at"]] = (q["s0"] * q["ge"]
                          + _dot(q["uv"].T.astype(BF16), (q["bk"] * q["ge"]).astype(BF16)))


def _rwkv_scan(r, v, kk, lw0, k0, e0, lw1, k1, e1, nctx_chunks):
    b, rows, d = r.shape
    c = CHUNK
    nc = rows // c
    fwd = lambda i: (0, i, 0)
    rev = lambda i: (0, jnp.where(i < nctx_chunks, nctx_chunks - 1 - i, nc - 1 - (i - nctx_chunks)), 0)
    sf = pl.BlockSpec((b, c, d), fwd)
    sr = pl.BlockSpec((b, c, d), rev)
    out = jax.ShapeDtypeStruct((b, rows, d), F32)
    return pl.pallas_call(
        _scan_kernel,
        grid=(nc,),
        in_specs=[sf] * 6 + [sr] * 6,
        out_specs=[sf, sr],
        out_shape=[out, out],
        scratch_shapes=[pltpu.VMEM((b, 2, d // (2 * HEAD), 2 * HEAD, 2 * HEAD), F32)],
        compiler_params=_cparams("arbitrary"),
    )(r, v, kk, lw0, k0, e0, r, v, kk, lw1, k1, e1)


def _post_mixer(z, x_ref, mod_ref, stream, wo_ref, lnv, rw_ref, x1_ref, h2_ref, afft_ref, acc_ref):
    n, tm, d = x_ref.shape
    o = _dot(z.astype(BF16), wo_ref[...])
    x1s, h2s = [], []
    for s in range(n):
        m = mod_ref[2 * s + stream]
        x1 = _layer_norm(lnv[2:3] * x_ref[s] + m[2:3] * o[s * tm:(s + 1) * tm], lnv[0:1], lnv[1:2])
        x1s.append(x1)
        h2s.append(x1 * (1 + m[4:5]) + m[3:4])
    h_hi, h_lo = _split(jnp.concatenate(h2s, axis=0))
    logits = _dot(jnp.concatenate([h_hi, h_lo, h_hi], axis=1), rw_ref[...])
    lane = lax.broadcasted_iota(I32, logits.shape, 1)
    logits = jnp.where(lane < N_EXPERTS, logits, NEG)
    ex = jnp.exp(logits - jnp.max(logits, axis=-1, keepdims=True))
    aff = ex / jnp.sum(ex, axis=-1, keepdims=True)
    acc_ref[...] = jnp.zeros(acc_ref.shape, F32)
    for s in range(n):
        aff_s = aff[s * tm:(s + 1) * tm]
        x1_ref[s] = x1s[s]
        h2_ref[s, :, :d] = h2s[s]
        h2_ref[s, :, d:] = aff_s
        afft_ref[s] = aff_s.T[:N_EXPERTS]


def _rwkv_out_kernel(nctx_blk, yf_ref, yr_ref, bonus_ref, g_ref, c_ref, x_ref, mod_ref, lnx_ref, wo_ref, lnv_ref,
                     rw_ref, x1_ref, h2_ref, afft_ref, acc_ref):
    n, tm, d = x_ref.shape
    rows = lambda ref: ref[...].reshape(n * tm, d)
    y = rows(yf_ref) + rows(yr_ref)
    mu = _head_sum(y) * (1.0 / HEAD)
    yc = y - mu
    var = _head_sum(yc * yc) * (1.0 / HEAD)
    lnx = lnx_ref[...]
    yn = yc * lax.rsqrt(var + RW_LNX_EPS) * lnx[0:1] + lnx[1:2]
    z = (yn + rows(bonus_ref)) * rows(g_ref)
    is_lat = pl.program_id(0) >= nctx_blk
    xs = jnp.where(is_lat, x_ref[...], c_ref[...])
    _post_mixer(z, xs, mod_ref, is_lat.astype(I32), wo_ref, lnv_ref[...], rw_ref, x1_ref, h2_ref, afft_ref,
                acc_ref)


def _post_mixer_outs(b, rows, d, tm):
    specs = [pl.BlockSpec((b, tm, d), lambda i: (0, i, 0)),
             pl.BlockSpec((b, tm, d + LANES), lambda i: (0, i, 0)),
             pl.BlockSpec((b, N_EXPERTS, tm), lambda i: (0, 0, i)),
             pl.BlockSpec((b, tm, d), lambda i: (0, i, 0))]
    shapes = [jax.ShapeDtypeStruct((b, rows, d), F32), jax.ShapeDtypeStruct((b, rows, d + LANES), F32),
              jax.ShapeDtypeStruct((b, N_EXPERTS, rows), F32), jax.ShapeDtypeStruct((b, rows, d), F32)]
    return specs, shapes


def _rwkv_out(yf, yr, bonus, g, ctx, x, mods, lnx, wo, lnv, rw_pad):
    b, r, d = yf.shape
    tm = ROW_TILE
    assert ctx.shape[1] == tm
    nctx_blk = 1
    row = pl.BlockSpec((b, tm, d), lambda i: (0, i, 0))
    full = lambda a: pl.BlockSpec(a.shape, lambda i: (0,) * a.ndim)
    out_specs, out_shape = _post_mixer_outs(b, r, d, tm)
    return pl.pallas_call(
        functools.partial(_rwkv_out_kernel, nctx_blk),
        grid=(r // tm,),
        in_specs=[row] * 4 + [pl.BlockSpec((b, tm, d), lambda i: (0, 0, 0)),
                              pl.BlockSpec((b, tm, d), lambda i: (0, jnp.maximum(i - nctx_blk, 0), 0)),
                              full(mods), full(lnx), full(wo), full(lnv), full(rw_pad)],
        out_specs=out_specs,
        out_shape=out_shape,
        compiler_params=_cparams("arbitrary"),
    )(yf, yr, bonus, g, ctx, x, mods, lnx, wo, lnv, rw_pad)


def _attn_out_kernel(o_ref, x_ref, mod_ref, wo_ref, lnv_ref, rw_ref, x1_ref, h2_ref, afft_ref, acc_ref):
    n, tm, d = x_ref.shape
    _post_mixer(o_ref[...].reshape(n * tm, d), x_ref, mod_ref, 1, wo_ref, lnv_ref[...], rw_ref,
                x1_ref, h2_ref, afft_ref, acc_ref)


def _attn_out(o, xa, mods, blk_off, wo, lnv, rw_pad):
    b, t, d = o.shape
    tm = ROW_TILE
    full = lambda a: pl.BlockSpec(a.shape, lambda i: (0,) * a.ndim)
    out_specs, out_shape = _post_mixer_outs(b, t, d, tm)
    return pl.pallas_call(
        _attn_out_kernel,
        grid=(t // tm,),
        in_specs=[pl.BlockSpec((b, tm, d), lambda i: (0, i, 0)),
                  pl.BlockSpec((b, tm, d), lambda i: (0, i + blk_off, 0)),
                  full(mods), full(wo), full(lnv), full(rw_pad)],
        out_specs=out_specs,
        out_shape=out_shape,
        compiler_params=_cparams("arbitrary"),
    )(o, xa, mods, wo, lnv, rw_pad)


def _select_kernel(cap, aff_ref, idx_ref, c_ref, bnd_v, bnd_s, sem):
    n_e, tn = aff_ref.shape[1], aff_ref.shape[2]

    def bit_step(j, bits):
        cand = bits | jnp.left_shift(jnp.int32(1), 30 - j)
        cnt = jnp.sum((aff_ref[0] >= lax.bitcast_convert_type(cand, F32)).astype(F32), axis=1, keepdims=True)
        return jnp.where(cnt >= cap, cand, bits)

    bits = lax.fori_loop(0, 31, bit_step, jnp.zeros((n_e, 1), I32))
    thr = lax.bitcast_convert_type(bits, F32)
    need = cap - jnp.sum((aff_ref[0] > thr).astype(F32), axis=1, keepdims=True)
    triu = (lax.broadcasted_iota(I32, (LANES, LANES), 0) <= lax.broadcasted_iota(I32, (LANES, LANES), 1)).astype(BF16)
    ns = min(cap, 256)
    tc = min(tn, 1024)
    lane = lax.broadcasted_iota(I32, (1, LANES), 1)
    tie_off = jnp.zeros((n_e, 1), F32)
    sel_off = jnp.zeros((n_e, 1), F32)
    bnd = jnp.zeros((n_e, LANES), F32)
    for j in range(tn // LANES):
        cols = slice(j * LANES, (j + 1) * LANES)
        a = aff_ref[0, :, cols]
        eq = (a == thr).astype(F32)
        tie_rank = _dot(eq.astype(BF16), triu) + tie_off - eq
        sel = ((a > thr) | ((eq > 0) & (tie_rank < need))).astype(F32)
        c_ref[:, cols] = _dot(sel.astype(BF16), triu) + sel_off
        tie_off = tie_off + jnp.sum(eq, axis=1, keepdims=True)
        sel_off = sel_off + jnp.sum(sel, axis=1, keepdims=True)
        if (j + 1) % (tc // LANES) == 0:
            bnd = jnp.where(lane == (j + 1) // (tc // LANES), sel_off, bnd)
    bnd_v[...] = bnd.astype(I32)
    to_smem = pltpu.make_async_copy(bnd_v, bnd_s, sem)
    to_smem.start()
    to_smem.wait()

    for ci in range(cap // ns):
        slot = (lax.broadcasted_iota(I32, (ns, 1), 0) + ci * ns).astype(F32)

        def per_expert(e, out):
            def chunk(j, part):
                def compare():
                    ce = c_ref[pl.ds(e, 1), pl.ds(pl.multiple_of(j * tc, tc), tc)]
                    acc = part
                    for k in range(tc // LANES):
                        acc = acc + (ce[:, k * LANES:(k + 1) * LANES] <= slot).astype(F32)
                    return acc

                before, upto = bnd_s[e, j], bnd_s[e, j + 1]
                return lax.cond(upto <= ci * ns, lambda: part + float(tc // LANES),
                                lambda: lax.cond(before > ci * ns + ns - 1, lambda: part, compare))

            part = lax.fori_loop(0, tn // tc, chunk, jnp.zeros((ns, LANES), F32))
            return jnp.where(lane == e, jnp.sum(part, axis=1, keepdims=True), out)

        out = lax.fori_loop(0, n_e, per_expert, jnp.zeros((ns, LANES), F32))
        idx_ref[0, ci * ns:(ci + 1) * ns, :] = out.astype(I32)


def _select(afft, cap):
    b, n_e, tn = afft.shape
    return pl.pallas_call(
        functools.partial(_select_kernel, cap),
        grid=(b,),
        in_specs=[pl.BlockSpec((1, n_e, tn), lambda bb: (bb, 0, 0))],
        out_specs=pl.BlockSpec((1, cap, LANES), lambda bb: (bb, 0, 0)),
        out_shape=jax.ShapeDtypeStruct((b, cap, LANES), I32),
        scratch_shapes=[pltpu.VMEM((n_e, tn), F32), pltpu.VMEM((n_e, LANES), I32), pltpu.SMEM((n_e, LANES), I32),
                        pltpu.SemaphoreType.DMA(())],
        compiler_params=_cparams("parallel"),
    )(afft)


def _expert_kernel(idx_ref, h_hbm, wg_ref, wu_ref, wd_ref, acc_in, acc_hbm,
                   stage, accst, xin, gate, yacc, sem_x, sem_a, sem_s):
    del acc_in
    e = pl.program_id(0)
    f = pl.program_id(1)
    n_e = pl.num_programs(0)
    nf, mb, dx = stage.shape
    m, d = xin.shape
    ns, ng = EXPERT_SCATTER_STEPS, EXPERT_GATHER_STEPS
    bs, bg = nf // ns, nf // ng
    base = e * m
    prev = base - m
    nxt = lax.rem(e + 1, n_e) * m

    def x_copy(first, blk, u):
        return pltpu.make_async_copy(h_hbm.at[pl.ds(idx_ref[first + blk * mb + u], 1)],
                                     stage.at[blk, pl.ds(u, 1)], sem_x)

    def a_copy(blk, u):
        return pltpu.make_async_copy(acc_hbm.at[pl.ds(idx_ref[base + blk * mb + u], 1)],
                                     accst.at[blk, pl.ds(u, 1)], sem_a)

    def s_copy(first, blk, u):
        return pltpu.make_async_copy(accst.at[blk, pl.ds(u, 1)],
                                     acc_hbm.at[pl.ds(idx_ref[first + blk * mb + u], 1)], sem_s)

    def looped(copy_of):
        for blk in range(nf):
            def body(i, carry):
                for u in range(DMA_UNROLL):
                    copy_of(blk, i * DMA_UNROLL + u).start()
                return carry

            lax.fori_loop(0, mb // DMA_UNROLL, body, 0)

    def wait_blocks(src_of, dst_of, sem):
        for blk in range(nf):
            pltpu.make_async_copy(src_of(blk), dst_of(blk), sem).wait()

    wait_x = lambda: wait_blocks(lambda blk: h_hbm.at[pl.ds(0, mb)], lambda blk: stage.at[blk], sem_x)
    wait_s = lambda: wait_blocks(lambda blk: accst.at[blk], lambda blk: acc_hbm.at[pl.ds(0, mb)], sem_s)

    @pl.when((e == 0) & (f == 0))
    def _():
        looped(lambda blk, u: x_copy(base, blk, u))
        wait_x()

    @pl.when(f == 0)
    def _():
        rows = stage[...]
        xin[...] = rows[:, :, :d].reshape(m, d).astype(BF16)
        lane = lax.broadcasted_iota(I32, (1, 1, LANES), 2)
        gate[...] = jnp.sum(jnp.where(lane == e, rows[:, :, d:], 0.0), axis=2, keepdims=True).reshape(m, 1)
        yacc[...] = jnp.zeros_like(yacc)

    @pl.when((f == ns) & (e > 0))
    def _():
        wait_s()

    def step(extra_copies):
        for u in range(mb):
            x_copy(nxt, f, u).start()
        extra_copies()
        x = xin[...]
        a = _dot(x, wg_ref[0, 0].astype(BF16))
        up = _dot(x, wu_ref[0, 0].astype(BF16))
        hm = (a * jax.nn.sigmoid(a) * up).astype(BF16)
        wd = wd_ref[0, 0].astype(BF16)
        rb = m // 4
        for r0 in range(0, m, rb):
            yacc[r0:r0 + rb] += _dot(hm[r0:r0 + rb], wd)

    def scatter_copies():
        for k in range(bs):
            for u in range(mb):
                s_copy(prev, f * bs + k, u).start()

    def gather_copies():
        for k in range(bg):
            for u in range(mb):
                a_copy((f - ns) * bg + k, u).start()

    scatters = (f < ns) & (e > 0)
    gathers = (f >= ns) & (f < ns + ng)
    pl.when(scatters)(lambda: step(scatter_copies))
    pl.when(gathers)(lambda: step(gather_copies))
    pl.when(jnp.logical_not(scatters | gathers))(lambda: step(lambda: None))

    @pl.when(f == nf - 1)
    def _():
        wait_blocks(lambda blk: acc_hbm.at[pl.ds(0, mb)], lambda blk: accst.at[blk], sem_a)
        accst[...] = accst[...] + (gate[...] * yacc[...]).reshape(nf, mb, d)
        wait_x()

    @pl.when((f == nf - 1) & (e == n_e - 1))
    def _():
        looped(lambda blk, u: s_copy(base, blk, u))
        wait_s()


def _experts(idx_flat, m, h_rows, acc, layer, w_gate, w_up, w_down):
    _, n_e, d, f = w_gate.shape
    tf = EXPERT_F_TILE
    nf = f // tf
    mb = m // nf
    assert mb * nf == m and mb % DMA_UNROLL == 0 and m % 64 == 0 and h_rows.shape[1] == d + LANES
    assert nf % EXPERT_SCATTER_STEPS == 0 and nf % EXPERT_GATHER_STEPS == 0
    assert EXPERT_SCATTER_STEPS + EXPERT_GATHER_STEPS <= nf
    grid_spec = pltpu.PrefetchScalarGridSpec(
        num_scalar_prefetch=1,
        grid=(n_e, f // tf),
        in_specs=[pl.BlockSpec(memory_space=pl.ANY),
                  pl.BlockSpec((1, 1, d, tf), lambda e, j, idx: (layer, e, 0, j)),
                  pl.BlockSpec((1, 1, d, tf), lambda e, j, idx: (layer, e, 0, j)),
                  pl.BlockSpec((1, 1, tf, d), lambda e, j, idx: (layer, e, j, 0)),
                  pl.BlockSpec(memory_space=pl.ANY)],
        out_specs=pl.BlockSpec(memory_space=pl.ANY),
        scratch_shapes=[pltpu.VMEM((nf, mb, d + LANES), F32), pltpu.VMEM((nf, mb, d), F32),
                        pltpu.VMEM((m, d), BF16), pltpu.VMEM((m, 1), F32), pltpu.VMEM((m, d), F32),
                        pltpu.SemaphoreType.DMA(()), pltpu.SemaphoreType.DMA(()), pltpu.SemaphoreType.DMA(())],
    )
    return pl.pallas_call(
        _expert_kernel,
        grid_spec=grid_spec,
        out_shape=jax.ShapeDtypeStruct(acc.shape, F32),
        input_output_aliases={5: 0},
        compiler_params=_cparams("arbitrary", "arbitrary"),
    )(idx_flat, h_rows, w_gate, w_up, w_down, acc)


def _moe_ln_kernel(x_ref, moe_ref, mod_ref, lnv_ref, o_ref):
    lnv = lnv_ref[...]
    m = mod_ref[0]
    o_ref[0] = _layer_norm(lnv[2:3] * x_ref[0] + m[5:6] * moe_ref[0], lnv[0:1], lnv[1:2])


def _moe_ln(x1, moe, mods, nctx_blk, lnv):
    b, r, d = x1.shape
    tm = ROW_TILE
    row = pl.BlockSpec((1, tm, d), lambda bb, i: (bb, i, 0))
    return pl.pallas_call(
        _moe_ln_kernel,
        grid=(b, r // tm),
        in_specs=[row, row,
                  pl.BlockSpec((1, N_MOD, d), lambda bb, i: (bb * 2 + (i >= nctx_blk).astype(I32), 0, 0)),
                  pl.BlockSpec(lnv.shape, lambda bb, i: (0, 0))],
        out_specs=row,
        out_shape=jax.ShapeDtypeStruct((b, r, d), F32),
        compiler_params=_cparams("parallel", "arbitrary"),
    )(x1, moe, mods, lnv)


def _ec_moe(h2x, afft, acc0, streams, layer, w_gate, w_up, w_down):
    b, r, dx = h2x.shape
    d = dx - LANES
    idx_parts = []
    for off, tn in streams:
        cap = EC_CAPACITY * tn // N_EXPERTS
        idx = _select(lax.slice_in_dim(afft, off, off + tn, axis=2), cap)[:, :, :N_EXPERTS]
        rows = idx + (jnp.arange(b, dtype=I32) * r + off)[:, None, None]
        idx_parts.append(jnp.transpose(rows, (2, 0, 1)).reshape(N_EXPERTS, b * cap))
    idx_flat = jnp.concatenate(idx_parts, axis=1)
    acc = _experts(idx_flat.reshape(-1), idx_flat.shape[1], h2x.reshape(b * r, dx),
                   acc0.reshape(b * r, d), layer, w_gate, w_up, w_down)
    return acc.reshape(b, r, d)


def _qkv_kernel(x_ref, mod_ref, cos_ref, sin_ref, wq_ref, wk_ref, wv_ref, q_ref, k_ref, v_ref):
    m = mod_ref[0]
    h = (x_ref[0] * (1 + m[1:2]) + m[0:1]).astype(BF16)
    d = x_ref.shape[2]
    rep = d // LANES
    cos = jnp.concatenate([cos_ref[...]] * rep, axis=1)
    sin = jnp.concatenate([sin_ref[...]] * rep, axis=1)
    lane = lax.broadcasted_iota(I32, (1, d), 1)
    lower = (lane % 32) < 16

    def rope(t):
        partner = jnp.where(lower, pltpu.roll(t, d - 16, 1), pltpu.roll(t, 16, 1))
        return t * cos + partner * sin

    q_ref[0] = (rope(_dot(h, wq_ref[...])) * (HEAD ** -0.5)).astype(BF16)
    k_ref[0] = rope(_dot(h, wk_ref[...])).astype(BF16)
    v_ref[0] = _dot(h, wv_ref[...]).astype(BF16)


def _qkv(xa, mods, nctx_blk, cos, sin, wq, wk4, wv4):
    b, r, d = xa.shape
    tm = ROW_TILE
    row = pl.BlockSpec((1, tm, d), lambda bb, i: (bb, i, 0))
    tab = pl.BlockSpec((tm, LANES), lambda bb, i: (i, 0))
    full = lambda a: pl.BlockSpec(a.shape, lambda bb, i: (0,) * a.ndim)
    out = jax.ShapeDtypeStruct((b, r, d), BF16)
    return pl.pallas_call(
        _qkv_kernel,
        grid=(b, r // tm),
        in_specs=[row, pl.BlockSpec((1, N_MOD, d), lambda bb, i: (bb * 2 + (i >= nctx_blk).astype(I32), 0, 0)),
                  tab, tab, full(wq), full(wk4), full(wv4)],
        out_specs=[row] * 3,
        out_shape=[out] * 3,
        compiler_params=_cparams("parallel", "arbitrary"),
    )(xa, mods, cos, sin, wq, wk4, wv4)


def _attn_kernel(n_ctx, sink_ref, q_ref, kc_ref, vc_ref, kp_ref, k0_ref, kn_ref, vp_ref, v0_ref, vn_ref, o_ref):
    nb = pl.program_id(1)
    n_blk = pl.num_programs(1)
    bq = q_ref.shape[1]
    gw = ATT_GROUP * HEAD
    nk = n_ctx + 3 * bq
    col = lax.broadcasted_iota(I32, (bq, nk), 1)
    qpos = nb * bq + lax.broadcasted_iota(I32, (bq, nk), 0)
    kpos = (nb - 1) * bq + (col - n_ctx)
    valid = (col < n_ctx) | ((kpos >= 0) & (kpos < n_blk * bq) & (jnp.abs(qpos - kpos) <= WINDOW))
    lane = lax.broadcasted_iota(I32, (1, gw), 1)
    sls = [slice(gw * h, gw * (h + 1)) for h in range(ATT_KV_HEADS)]
    scores, probs = {}, {}

    def score_stage(h):
        sl = sls[h]
        keys = jnp.concatenate([kc_ref[0, :, sl], kp_ref[0, :, sl], k0_ref[0, :, sl], kn_ref[0, :, sl]], axis=0)
        qh = q_ref[0, :, sl]
        scores[h] = [_dot_nt(jnp.where((lane // HEAD) == g, qh, jnp.zeros_like(qh)), keys)
                     for g in range(ATT_GROUP)]

    def softmax_stage(h):
        probs[h] = []
        for g, s in enumerate(scores.pop(h)):
            s = jnp.where(valid, s, NEG)
            sink = sink_ref[h * ATT_GROUP + g]
            mx = jnp.maximum(jnp.max(s, axis=-1, keepdims=True), sink)
            p = jnp.exp(s - mx)
            den = jnp.sum(p, axis=-1, keepdims=True) + jnp.exp(sink - mx)
            probs[h].append((p.astype(BF16), 1.0 / den))

    def value_stage(h):
        sl = sls[h]
        sl2 = slice(gw * h, gw * h + 2 * HEAD)
        vals = jnp.concatenate([vc_ref[0, :, sl2], vp_ref[0, :, sl2], v0_ref[0, :, sl2], vn_ref[0, :, sl2]], axis=0)
        low = lane[:, :2 * HEAD] < HEAD
        zero = jnp.zeros_like(vals)
        vals2 = jnp.concatenate([jnp.where(low, vals, zero), jnp.where(low, zero, vals)], axis=0)
        pg = probs.pop(h)
        for g in range(0, ATT_GROUP, 2):
            (p0, inv0), (p1, inv1) = pg[g], pg[g + 1]
            o2 = _dot(jnp.concatenate([p0, p1], axis=1), vals2) * jnp.where(low, inv0, inv1)
            o_ref[0, :, gw * h + g * HEAD:gw * h + (g + 2) * HEAD] = o2.astype(BF16)

    for step in range(ATT_KV_HEADS + 2):
        if step < ATT_KV_HEADS:
            score_stage(step)
        if 1 <= step <= ATT_KV_HEADS:
            softmax_stage(step - 1)
        if step >= 2:
            value_stage(step - 2)


def _attention(q, k4, v4, sink, n_ctx):
    b, r, d = q.shape
    bq = WINDOW
    t = r - n_ctx
    nb = t // bq
    off = n_ctx // bq
    blk = lambda f: pl.BlockSpec((1, bq, d), f)
    ctx_spec = pl.BlockSpec((1, n_ctx, d), lambda bb, i: (bb, 0, 0))
    prev = lambda bb, i: (bb, jnp.maximum(i - 1, 0) + off, 0)
    cur = lambda bb, i: (bb, i + off, 0)
    nxt = lambda bb, i: (bb, jnp.minimum(i + 1, nb - 1) + off, 0)
    return pl.pallas_call(
        functools.partial(_attn_kernel, n_ctx),
        grid=(b, nb),
        in_specs=[pl.BlockSpec(memory_space=pltpu.SMEM), blk(cur), ctx_spec, ctx_spec,
                  blk(prev), blk(cur), blk(nxt), blk(prev), blk(cur), blk(nxt)],
        out_specs=pl.BlockSpec((1, bq, d), lambda bb, i: (bb, i, 0)),
        out_shape=jax.ShapeDtypeStruct((b, t, d), BF16),
        compiler_params=_cparams("parallel", "arbitrary"),
    )(sink, q, k4, v4, k4, k4, k4, v4, v4, v4)


def _rope_tables(n_ctx, t):
    half = HEAD // 2
    inv = 1.0 / (ROPE_BASE ** (jnp.arange(0, half, 2, dtype=F32) / half))
    pos = jnp.arange(t)
    ang_r = (pos // GRID_W).astype(F32)[:, None] * inv
    ang_c = (pos % GRID_W).astype(F32)[:, None] * inv
    cos = jnp.concatenate([jnp.cos(ang_r)] * 2 + [jnp.cos(ang_c)] * 2, axis=1)
    sin = jnp.concatenate([-jnp.sin(ang_r), jnp.sin(ang_r), -jnp.sin(ang_c), jnp.sin(ang_c)], axis=1)
    cos = jnp.concatenate([jnp.ones((n_ctx, HEAD), F32), cos], axis=0)
    sin = jnp.concatenate([jnp.zeros((n_ctx, HEAD), F32), sin], axis=0)
    return jnp.tile(cos, (1, 2)), jnp.tile(sin, (1, 2))


def kernel(x, c, ctx, c_ctx, ada_w, ada_b, ln1_g, ln1_b, ln2_g, ln2_b, rw_mu, rw_wr, rw_wk, rw_wv, rw_w0, rw_w1, rw_w2, rw_a0, rw_a1, rw_a2, rw_g1, rw_g2, rw_kk, rw_ka, rw_rk, rw_lnx_g, rw_lnx_b, rw_wo, at_wqkv, at_wo, at_sink, router_w, moe_w_gate, moe_w_up, moe_w_down):
    b, t, d = x.shape
    n_ctx = ctx.shape[1]
    depth = ada_w.shape[0]
    alpha = (2 * depth) ** 0.25
    assert n_ctx % ROW_TILE == 0 and t % ROW_TILE == 0 and b + 1 <= 8
    nctx_blk = n_ctx // ROW_TILE

    cc = jnp.concatenate([c, c_ctx[None], jnp.zeros((8 - b - 1, d), F32)], axis=0)
    mod_all = _ada_mod(cc, ada_w, ada_b).reshape(depth, 8, N_MOD, d)
    xa = None
    bf = lambda w: w.astype(BF16)
    pad_rows = lambda w, lo: jnp.pad(w, ((lo, LANES - lo - w.shape[0]), (0, 0)))

    for i in range(depth):
        last = i == depth - 1
        j = i // 2
        mods = jnp.stack([jnp.broadcast_to(mod_all[i, b], (b, N_MOD, d)), mod_all[i, :b]], axis=1).reshape(2 * b, N_MOD, d)
        lnv1 = jnp.stack([ln1_g[i], ln1_b[i], jnp.full((d,), alpha, F32)])
        lnv2 = jnp.stack([ln2_g[i], ln2_b[i], jnp.full((d,), alpha, F32)])
        rw_hi, rw_lo = _split(jnp.pad(router_w[i], ((0, 0), (0, LANES - N_EXPERTS))))
        rw_pad = jnp.concatenate([rw_hi, rw_hi, rw_lo], axis=0)
        if i % 2 == 0:
            vec = jnp.concatenate([rw_mu[j], rw_w0[j], rw_a0[j], rw_kk[j][None], rw_ka[j][None],
                                   rw_rk[j].reshape(1, d), jnp.zeros((3, d), F32)], axis=0)
            w1 = bf(jnp.concatenate([rw_w1[j, 0], rw_w1[j, 1]], axis=1))
            a1 = bf(jnp.concatenate([rw_a1[j, 0], rw_a1[j, 1]], axis=1))
            w2 = bf(jnp.stack([pad_rows(rw_w2[j, 0], 0), pad_rows(rw_w2[j, 1], HEAD)]))
            a2 = bf(jnp.stack([pad_rows(rw_a2[j, 0], 0), pad_rows(rw_a2[j, 1], HEAD)]))
            if xa is not None:
                ctx, x = xa[:, :n_ctx], xa[:, n_ctx:]
            (r, v, kk, g, bonus, lw0, k0, e0, lw1, k1, e1) = _rwkv_proj(
                ctx, x, mods, vec, bf(rw_wr[j]), bf(rw_wk[j]), bf(rw_wv[j]), w1, a1, bf(rw_g1[j]),
                w2, a2, bf(rw_g2[j]))
            yf, yr = _rwkv_scan(r, v, kk, lw0, k0, e0, lw1, k1, e1, n_ctx // CHUNK)
            lnx = jnp.stack([rw_lnx_g[j], rw_lnx_b[j]])
            x1, h2, afft, acc0 = _rwkv_out(yf, yr, bonus, g, ctx, x, mods, lnx, bf(rw_wo[j]), lnv1, rw_pad)
        else:
            if xa is None:
                xa = jnp.concatenate([ctx, x], axis=1)
            nq = ATT_KV_HEADS * ATT_GROUP * HEAD
            nk = ATT_KV_HEADS * HEAD
            wqkv = at_wqkv[j]
            tile4 = lambda w: jnp.tile(w.reshape(d, ATT_KV_HEADS, 1, HEAD), (1, 1, ATT_GROUP, 1)).reshape(d, nq)
            cos, sin = _rope_tables(n_ctx, t)
            q, k4, v4 = _qkv(xa, mods, nctx_blk, cos, sin, bf(wqkv[:, :nq]),
                             bf(tile4(wqkv[:, nq:nq + nk])), bf(tile4(wqkv[:, nq + nk:])))
            o = _attention(q, k4, v4, at_sink[j], n_ctx)
            x1, h2, afft, acc0 = _attn_out(o, xa, mods, nctx_blk, bf(at_wo[j]), lnv1, rw_pad)
        if last:
            if x1.shape[1] != t:
                x1, h2, afft, acc0 = x1[:, n_ctx:], h2[:, n_ctx:], afft[:, :, n_ctx:], acc0[:, n_ctx:]
            moe = _ec_moe(h2, afft, acc0, [(0, t)], i, moe_w_gate, moe_w_up, moe_w_down)
            return _moe_ln(x1, moe, mods, 0, lnv2)
        moe = _ec_moe(h2, afft, acc0, [(n_ctx, t), (0, n_ctx)], i, moe_w_gate, moe_w_up, moe_w_down)
        xa = _moe_ln(x1, moe, mods, nctx_blk, lnv2)
    return xa[:, n_ctx:]
```
